```python
import functools
import jax, jax.numpy as jnp
from jax import lax
import numpy as np

D_MODEL = 1024
BATCH = 2
SEQ = 8192
DEPTH = 2
DEC_BATCH = 32
DEC_SEQ = 4
PAST_LEN = 16384
PAGE_SIZE = 128

HEAD_DIM = 64
MIX_WIDTH = D_MODEL
W_A = MIX_WIDTH // 2
N_HEADS_A = W_A // HEAD_DIM
W_B = MIX_WIDTH // 4
W_C = MIX_WIDTH - W_A - W_B
N_GROUPS_C = W_C // HEAD_DIM
GROUP_DIM_C = W_C // N_GROUPS_C
CONV_W = 31
CHUNK = 128
QBLOCK = 128
FFN_CONV_W = 3
D_FF = ((8 * D_MODEL // 3 + 255) // 256) * 256
EPS = 1e-6
SB_BIAS_NEAR = -4.0
SB_BIAS_FAR = -10.0
SPLIT_POINTS = [W_A, 2 * W_A, 3 * W_A, 3 * W_A + W_B, 3 * W_A + 2 * W_B, 3 * W_A + 2 * W_B + W_C]
IN_WIDTH = 3 * W_A + 2 * W_B + 2 * W_C

kernel_name = 'hybrid_stickbreak_conformer_gmlp_step'


def _rms_norm(x, g):
    xf = x.astype(jnp.float32)
    y = xf * lax.rsqrt(jnp.mean(xf * xf, axis=-1, keepdims=True) + EPS)
    return (y * g).astype(x.dtype)


def _layer_norm(x, g, b):
    xf = x.astype(jnp.float32)
    xc = xf - jnp.mean(xf, axis=-1, keepdims=True)
    y = xc * lax.rsqrt(jnp.mean(xc * xc, axis=-1, keepdims=True) + EPS)
    return (y * g + b).astype(x.dtype)


def _causal_dwconv(x, hist, w, b):
    xc = jnp.concatenate([hist.astype(x.dtype), x], axis=1)
    y = lax.conv_general_dilated(xc, w[:, None, :].astype(x.dtype), window_strides=(1,), padding='VALID',
                                 dimension_numbers=('NWC', 'WIO', 'NWC'), feature_group_count=x.shape[-1])
    return y + b, xc[:, -(w.shape[0] - 1):]


def _sb_chunk(q, k, v, bias, mask, carry):
    z = (jnp.einsum('bqhd,bkhd->bhqk', q, k).astype(jnp.float32) * (HEAD_DIM ** -0.5)
         + bias.astype(jnp.float32)[None, :, None, None])
    log_stay = jnp.where(mask, jax.nn.log_sigmoid(-z), 0.0)
    suffix = lax.cumsum(log_stay, axis=3, reverse=True) - log_stay + carry[..., None]
    w = jnp.where(mask, jnp.exp(jax.nn.log_sigmoid(z) + suffix), 0.0)
    o = jnp.einsum('bhqk,bkhd->bqhd', w.astype(v.dtype), v)
    return o, carry + jnp.sum(log_stay, axis=3)


def _sb_prompt(q, k, v, bias):
    B, T, H, dh = q.shape
    nb = T // QBLOCK
    qb = q.reshape(B, nb, QBLOCK, H, dh).transpose(1, 0, 2, 3, 4)
    kpos = jnp.arange(T)

    def one_block(args):
        i, q_blk = args
        qpos = i * QBLOCK + jnp.arange(QBLOCK)
        o, _ = _sb_chunk(q_blk, k, v, bias, kpos[None, :] < qpos[:, None],
                         jnp.zeros((B, H, QBLOCK), jnp.float32))
        return o

    ob = lax.map(one_block, (jnp.arange(nb), qb))
    return ob.transpose(1, 0, 2, 3, 4).reshape(B, T, H, dh)


def _sb_sample(q, k, v, bias, k_pool, v_pool, page_table):
    B, T, H, dh = q.shape
    t = jnp.arange(T)
    o_new, carry = _sb_chunk(q, k, v, bias, t[None, :] < t[:, None], jnp.zeros((B, H, T), jnp.float32))
    mask_past = jnp.ones((T, k_pool.shape[1]), dtype=bool)

    def page_step(acc, phys):
        o_acc, c = acc
        o, c = _sb_chunk(q, k_pool[phys], v_pool[phys], bias, mask_past, c)
        return (o_acc + o.astype(jnp.float32), c), None

    (o, _), _ = lax.scan(page_step, (o_new.astype(jnp.float32), carry), page_table.T, reverse=True)
    return o.astype(q.dtype)


def _chunk_spatial(v, w_s, b_s):
    B, T, _ = v.shape
    L = min(T, CHUNK)
    vr = v.reshape(B, T // L, L, N_GROUPS_C, GROUP_DIM_C)
    wm = jnp.tril(w_s[:, :L, :L])
    out = jnp.einsum('gts,bnsgd->bntgd', wm, vr) + b_s[:, :L].T[:, :, None]
    return out.reshape(B, T, W_C)


def _layer(x, c, conv_hist, ffn_hist, attend, ada_w, ada_b, norm1_g, norm2_g, w_in, q_norm_g, k_norm_g,
           sb_bias, conv_w, conv_b, conv_ln_g, conv_ln_b, sg_ln_g, sg_ln_b, sg_w, sg_b, w_o,
           ffn_wg, ffn_wu, ffn_conv_w, ffn_conv_b, ffn_wd):
    B, T, _ = x.shape
    mod = (jax.nn.silu(c) @ ada_w + ada_b)[:, None, :]
    sh1, sc1, g1, sh2, sc2, g2 = jnp.split(mod, 6, axis=-1)
    h = _rms_norm(x, norm1_g) * (1 + sc1) + sh1
    q, k, v, b_val, b_gate, c_u, c_v = jnp.split(h @ w_in, SPLIT_POINTS, axis=-1)
    q = _rms_norm(q.reshape(B, T, N_HEADS_A, HEAD_DIM), q_norm_g)
    k = _rms_norm(k.reshape(B, T, N_HEADS_A, HEAD_DIM), k_norm_g)
    v = v.reshape(B, T, N_HEADS_A, HEAD_DIM)
    o_a = attend(q, k, v, sb_bias).reshape(B, T, W_A)
    glu = b_val * jax.nn.sigmoid(b_gate)
    conv_out, conv_tail = _causal_dwconv(glu, conv_hist, conv_w, conv_b)
    o_b = jax.nn.silu(_layer_norm(conv_out, conv_ln_g, conv_ln_b))
    v_sg = _layer_norm(jax.nn.gelu(c_v), sg_ln_g, sg_ln_b)
    o_c = jax.nn.gelu(c_u) * _chunk_spatial(v_sg, sg_w, sg_b)
    x = x + g1 * (jnp.concatenate([o_a, o_b, o_c], axis=-1) @ w_o)
    h2 = _rms_norm(x, norm2_g) * (1 + sc2) + sh2
    gate_conv, ffn_tail = _causal_dwconv(h2 @ ffn_wg, ffn_hist, ffn_conv_w, ffn_conv_b)
    x = x + g2 * ((jax.nn.silu(gate_conv) * (h2 @ ffn_wu)) @ ffn_wd)
    return x, k, v, conv_tail, ffn_tail, v_sg


def setup_inputs(seed: int = 0) -> dict:
    key = jax.random.key(seed)
    ks = jax.random.split(key, 32)
    f32 = jnp.float32
    n_pages = PAST_LEN // PAGE_SIZE
    n_used = DEC_BATCH * n_pages
    n_pool = n_used + n_used // 4

    def nrm(k, shape, scale):
        return jax.random.normal(k, shape, f32) * scale

    page_table = jax.random.permutation(ks[0], n_pool)[:n_used].reshape(DEC_BATCH, n_pages).astype(jnp.int32)
    sb_bias = jnp.linspace(SB_BIAS_NEAR, SB_BIAS_FAR, N_HEADS_A, dtype=f32)[None, :] + nrm(ks[30], (DEPTH, N_HEADS_A), 0.1)
    return {
        'x_prompt': nrm(ks[1], (BATCH, SEQ, D_MODEL), 1.0),
        'x_sample': nrm(ks[2], (DEC_BATCH, DEC_SEQ, D_MODEL), 1.0),
        'cache_k': nrm(ks[3], (DEPTH, n_pool, PAGE_SIZE, N_HEADS_A, HEAD_DIM), 1.0),
        'cache_v': nrm(ks[4], (DEPTH, n_pool, PAGE_SIZE, N_HEADS_A, HEAD_DIM), 1.0),
        'state_conv': nrm(ks[5], (DEPTH, DEC_BATCH, CONV_W - 1, W_B), 0.5),
        'state_ffn': nrm(ks[6], (DEPTH, DEC_BATCH, FFN_CONV_W - 1, D_FF), 0.5),
        'page_table': page_table,
        'c_prompt': nrm(ks[7], (BATCH, D_MODEL), 1.0),
        'c_sample': nrm(ks[8], (DEC_BATCH, D_MODEL), 1.0),
        'ada_w': nrm(ks[9], (DEPTH, D_MODEL, 6 * D_MODEL), 0.5 * D_MODEL ** -0.5),
        'ada_b': nrm(ks[10], (DEPTH, 6 * D_MODEL), 0.02),
        'norm1_g': 1.0 + nrm(ks[11], (DEPTH, D_MODEL), 0.02),
        'norm2_g': 1.0 + nrm(ks[12], (DEPTH, D_MODEL), 0.02),
        'w_in': nrm(ks[13], (DEPTH, D_MODEL, IN_WIDTH), D_MODEL ** -0.5),
        'q_norm_g': 1.0 + nrm(ks[14], (DEPTH, HEAD_DIM), 0.02),
        'k_norm_g': 1.0 + nrm(ks[15], (DEPTH, HEAD_DIM), 0.02),
        'sb_bias': sb_bias,
        'conv_w': nrm(ks[16], (DEPTH, CONV_W, W_B), CONV_W ** -0.5),
        'conv_b': nrm(ks[17], (DEPTH, W_B), 0.02),
        'conv_ln_g': 1.0 + nrm(ks[18], (DEPTH, W_B), 0.02),
        'conv_ln_b': nrm(ks[19], (DEPTH, W_B), 0.02),
        'sg_ln_g': 1.0 + nrm(ks[20], (DEPTH, W_C), 0.02),
        'sg_ln_b': nrm(ks[21], (DEPTH, W_C), 0.02),
        'sg_w': nrm(ks[22], (DEPTH, N_GROUPS_C, CHUNK, CHUNK), CHUNK ** -0.5),
        'sg_b': 1.0 + nrm(ks[23], (DEPTH, N_GROUPS_C, CHUNK), 0.02),
        'w_o': nrm(ks[24], (DEPTH, MIX_WIDTH, D_MODEL), MIX_WIDTH ** -0.5),
        'ffn_wg': nrm(ks[25], (DEPTH, D_MODEL, D_FF), D_MODEL ** -0.5),
        'ffn_wu': nrm(ks[26], (DEPTH, D_MODEL, D_FF), D_MODEL ** -0.5),
        'ffn_conv_w': nrm(ks[27], (DEPTH, FFN_CONV_W, D_FF), FFN_CONV_W ** -0.5),
        'ffn_conv_b': nrm(ks[28], (DEPTH, D_FF), 0.02),
        'ffn_wd': nrm(ks[29], (DEPTH, D_FF, D_MODEL), D_FF ** -0.5),
    }


def reference(x_prompt, x_sample, cache_k, cache_v, state_conv, state_ffn, page_table, c_prompt, c_sample,
              ada_w, ada_b, norm1_g, norm2_g, w_in, q_norm_g, k_norm_g, sb_bias, conv_w, conv_b, conv_ln_g,
              conv_ln_b, sg_ln_g, sg_ln_b, sg_w, sg_b, w_o, ffn_wg, ffn_wu, ffn_conv_w, ffn_conv_b, ffn_wd):
    nb = x_prompt.shape[0]
    conv0 = jnp.zeros((nb, CONV_W - 1, W_B), x_prompt.dtype)
    ffn0 = jnp.zeros((nb, FFN_CONV_W - 1, D_FF), x_prompt.dtype)
    yp, ys = x_prompt, x_sample
    kp, vp, ks_, vs_, cp, cs, fp, fs, gs = [], [], [], [], [], [], [], [], []
    for l in range(DEPTH):
        w = (ada_w[l], ada_b[l], norm1_g[l], norm2_g[l], w_in[l], q_norm_g[l], k_norm_g[l], sb_bias[l],
             conv_w[l], conv_b[l], conv_ln_g[l], conv_ln_b[l], sg_ln_g[l], sg_ln_b[l], sg_w[l], sg_b[l], w_o[l],
             ffn_wg[l], ffn_wu[l], ffn_conv_w[l], ffn_conv_b[l], ffn_wd[l])
        yp, k_l, v_l, c_l, f_l, _ = _layer(yp, c_prompt, conv0, ffn0, _sb_prompt, *w)
        kp.append(k_l); vp.append(v_l); cp.append(c_l); fp.append(f_l)
        attend_s = functools.partial(_sb_sample, k_pool=cache_k[l], v_pool=cache_v[l], page_table=page_table)
        ys, k_l, v_l, c_l, f_l, g_l = _layer(ys, c_sample, state_conv[l], state_ffn[l], attend_s, *w)
        ks_.append(k_l); vs_.append(v_l); cs.append(c_l); fs.append(f_l); gs.append(g_l)
    return (yp, ys, jnp.stack(kp), jnp.stack(vp), jnp.stack(ks_), jnp.stack(vs_),
            jnp.stack(cp), jnp.stack(cs), jnp.stack(fp), jnp.stack(fs), jnp.stack(gs))
```

```python
import functools

import jax
import jax.numpy as jnp
from jax import lax
from jax.experimental import pallas as pl
from jax.experimental.pallas import tpu as pltpu

F32 = jnp.float32
BF16 = jnp.bfloat16

D_MODEL = 1024
DEPTH = 2
HEAD_DIM = 64
W_A = 512
N_HEADS = 8
W_B = 256
W_C = 256
N_GROUPS_C = 4
CONV_W = 31
CHUNK = 128
QBLOCK = 128
FFN_CONV_W = 3
D_FF = 2816
IN_WIDTH = 3 * W_A + 2 * W_B + 2 * W_C
EPS = 1e-6
PAGE = 128

LANES = 128
VMEM_LIMIT = 56 * 1024 * 1024


def _params(n_axes, vmem=None):
    return pltpu.CompilerParams(dimension_semantics=("arbitrary",) * n_axes,
                                vmem_limit_bytes=vmem)


def _full(shape):
    return pl.BlockSpec(shape, lambda *_: (0,) * len(shape))


def _sigmoid(x):
    return 1.0 / (1.0 + jnp.exp(-x))


def _gelu_tanh(x):
    c = 0.7978845608028654
    return x * (0.5 * (1.0 + jnp.tanh(c * (x + 0.044715 * (x * x * x)))))


def _layer_norm(x, g, b):
    xc = x - jnp.mean(x, axis=-1, keepdims=True)
    return xc * lax.rsqrt(jnp.mean(xc * xc, axis=-1, keepdims=True) + EPS) * g + b


def _softplus(z):
    return jnp.maximum(z, 0.0) + jnp.log1p(jnp.exp(-jnp.abs(z)))


def _mod_kernel(c_ref, w_ref, b_ref, o_ref):
    c = c_ref[...]
    s = (c * _sigmoid(c)).astype(BF16)
    o_ref[0] = jnp.dot(s, w_ref[0].astype(BF16), preferred_element_type=F32) + b_ref[0]


def _modulation(c_all, ada_w, ada_b):
    rows = c_all.shape[0]
    tn = 1536
    return pl.pallas_call(
        _mod_kernel,
        grid=(DEPTH, 6 * D_MODEL // tn),
        in_specs=[_full((rows, D_MODEL)),
                  pl.BlockSpec((1, D_MODEL, tn), lambda l, n: (l, 0, n)),
                  pl.BlockSpec((1, 1, tn), lambda l, n: (l, 0, n))],
        out_specs=pl.BlockSpec((1, rows, tn), lambda l, n: (l, 0, n)),
        out_shape=jax.ShapeDtypeStruct((DEPTH, rows, 6 * D_MODEL), F32),
        compiler_params=_params(2, VMEM_LIMIT),
        name="modulation",
    )(c_all, ada_w, ada_b.reshape(DEPTH, 1, 6 * D_MODEL))


def _pre_kernel(x_ref, sh_ref, sc_ref, g_ref, win_ref, qg_ref, kg_ref, gmat_ref, lng_ref, lnb_ref,
                q_ref, kf_ref, vf_ref, kb_ref, vb_ref, glu_ref, gu_ref, vsg_ref):
    x = x_ref[0]
    y = x * lax.rsqrt(jnp.mean(x * x, axis=-1, keepdims=True) + EPS) * g_ref[...]
    h = (y * (1.0 + sc_ref[0]) + sh_ref[0]).astype(BF16)

    def proj(lo, hi):
        return jnp.dot(h, win_ref[:, lo:hi], preferred_element_type=F32)

    def head_norm(t, g):
        ms = jnp.dot((t * t).astype(BF16), gmat_ref[...], preferred_element_type=F32)
        return t * lax.rsqrt(ms + EPS) * g

    q = head_norm(proj(0, W_A), qg_ref[...])
    q_ref[0] = (q * (HEAD_DIM ** -0.5)).astype(BF16)
    k = head_norm(proj(W_A, 2 * W_A), kg_ref[...])
    kf_ref[0] = k
    kb_ref[0] = k.astype(BF16)
    v = proj(2 * W_A, 3 * W_A)
    vf_ref[0] = v
    vb_ref[0] = v.astype(BF16)
    o = 3 * W_A
    glu_ref[0] = proj(o, o + W_B) * _sigmoid(proj(o + W_B, o + 2 * W_B))
    o += 2 * W_B
    gu_ref[0] = _gelu_tanh(proj(o, o + W_C))
    vsg_ref[0] = _layer_norm(_gelu_tanh(proj(o + W_C, o + 2 * W_C)), lng_ref[...], lnb_ref[...])


def _pre(x, sh, sc, norm_g, w_in, qg, kg, gmat, lng, lnb, tm):
    bx, tx, _ = x.shape
    rm = sh.shape[1]
    mod_spec = (pl.BlockSpec((1, 1, D_MODEL), lambda b, i: (b, 0, 0)) if rm == 1 else
                pl.BlockSpec((1, tm, D_MODEL), lambda b, i: (b, i, 0)))

    def row(w):
        return pl.BlockSpec((1, tm, w), lambda b, i: (b, i, 0))

    widths = (W_A, W_A, W_A, W_A, W_A, W_B, W_C, W_C)
    dtypes = (BF16, F32, F32, BF16, BF16, F32, F32, F32)
    return pl.pallas_call(
        _pre_kernel,
        grid=(bx, tx // tm),
        in_specs=[row(D_MODEL), mod_spec, mod_spec, _full((1, D_MODEL)), _full((D_MODEL, IN_WIDTH)),
                  _full((1, W_A)), _full((1, W_A)), _full((W_A, W_A)), _full((1, W_C)), _full((1, W_C))],
        out_specs=[row(w) for w in widths],
        out_shape=[jax.ShapeDtypeStruct((bx, tx, w), d) for w, d in zip(widths, dtypes)],
        compiler_params=_params(2, VMEM_LIMIT),
        name="pre",
    )(x, sh, sc, norm_g, w_in, qg, kg, gmat, lng, lnb)


def _attn_prompt_kernel(bias_ref, q_ref, k_ref, v_ref, ss_ref, o_ref):
    p = pl.program_id(1)
    i = pl.program_id(2)
    q = q_ref[0].astype(F32)
    lane = lax.broadcasted_iota(jnp.int32, (QBLOCK, LANES), 1)
    row = lax.broadcasted_iota(jnp.int32, (QBLOCK, LANES), 0)
    first = lane < HEAD_DIM
    causal = lane < row
    ss = ss_ref[...]

    def block(qh, bias, j, carry, acc, mask):
        start = pl.multiple_of(j * QBLOCK, QBLOCK)
        kblk = k_ref[0, pl.ds(start, QBLOCK), :]
        vblk = v_ref[0, pl.ds(start, QBLOCK), :]
        z = lax.dot_general(qh, kblk, (((1,), (1,)), ((), ())), preferred_element_type=F32) + bias
        ls = -_softplus(z)
        if mask is not None:
            ls = jnp.where(mask, ls, 0.0)
        hi = ls.astype(BF16)
        lo = (ls - hi.astype(F32)).astype(BF16)
        res = jnp.dot(jnp.concatenate([hi, lo], axis=1), ss, preferred_element_type=F32)
        w = jnp.exp(z + ls + res[:, :QBLOCK] + carry)
        if mask is not None:
            w = jnp.where(mask, w, 0.0)
        acc = acc + jnp.dot(w.astype(BF16), vblk, preferred_element_type=F32)
        return carry + res[:, QBLOCK:], acc

    def head(qh, bias):
        zeros = jnp.zeros((QBLOCK, LANES), F32)
        carry, acc = block(qh, bias, i, zeros, zeros, causal)

        def body(it, st):
            return block(qh, bias, i - 1 - it, st[0], st[1], None)

        return lax.fori_loop(0, i, body, (carry, acc))[1]

    acc_a = head(jnp.where(first, q, 0.0).astype(BF16), bias_ref[2 * p])
    acc_b = head(jnp.where(first, 0.0, q).astype(BF16), bias_ref[2 * p + 1])
    o_ref[0] = jnp.where(first, acc_a, acc_b)


def _attn_prompt(q, k, v, bias, ss):
    b, t, _ = q.shape
    grid_spec = pltpu.PrefetchScalarGridSpec(
        num_scalar_prefetch=1,
        grid=(b, W_A // LANES, t // QBLOCK),
        in_specs=[pl.BlockSpec((1, QBLOCK, LANES), lambda b, p, i, s: (b, i, p)),
                  pl.BlockSpec((1, t, LANES), lambda b, p, i, s: (b, 0, p)),
                  pl.BlockSpec((1, t, LANES), lambda b, p, i, s: (b, 0, p)),
                  pl.BlockSpec((2 * QBLOCK, 2 * QBLOCK), lambda b, p, i, s: (0, 0))],
        out_specs=pl.BlockSpec((1, QBLOCK, LANES), lambda b, p, i, s: (b, i, p)),
    )
    return pl.pallas_call(
        _attn_prompt_kernel,
        grid_spec=grid_spec,
        out_shape=jax.ShapeDtypeStruct((b, t, W_A), F32),
        compiler_params=_params(3, VMEM_LIMIT),
        name="attn_prompt",
    )(bias, q, k, v, ss)


def _rev_cumsum_lanes(x):
    lane = lax.broadcasted_iota(jnp.int32, x.shape, 1)
    d = 1
    while d < LANES:
        x = x + jnp.where(lane + d < LANES, pltpu.roll(x, LANES - d, axis=1), 0.0)
        d *= 2
    return x


def _attn_sample_kernel(pt_ref, qbd_ref, bias_ref, kn_ref, vn_ref, *refs, pp, n_steps):
    del pt_ref
    k_refs, v_refs = refs[:pp], refs[pp:2 * pp]
    o_ref, acc_ref, carry_ref = refs[2 * pp:]
    s = pl.program_id(1)
    qbd = qbd_ref[0]
    bias = bias_ref[...]
    nrow = qbd.shape[0]
    lane = lax.broadcasted_iota(jnp.int32, (nrow, LANES), 1)
    row = lax.broadcasted_iota(jnp.int32, (nrow, LANES), 0)

    @pl.when(s == 0)
    def _():
        pad = jnp.zeros((PAGE - 8, W_A), F32)
        kn = jnp.concatenate([kn_ref[0], pad], axis=0).astype(BF16)
        vn = jnp.concatenate([vn_ref[0], pad], axis=0).astype(BF16)
        mask = lane < row // N_HEADS
        z = lax.dot_general(qbd, kn, (((1,), (1,)), ((), ())), preferred_element_type=F32) + bias
        ls = jnp.where(mask, -_softplus(z), 0.0)
        incl = _rev_cumsum_lanes(ls)
        w = jnp.where(mask, jnp.exp(z + incl), 0.0)
        acc_ref[...] = jnp.dot(w.astype(BF16), vn, preferred_element_type=F32)
        carry_ref[...] = jnp.broadcast_to(jnp.sum(ls, axis=1, keepdims=True), (nrow, LANES))

    kcat = jnp.concatenate([r[0, 0].astype(BF16) for r in k_refs], axis=1)
    vcat = jnp.concatenate([r[0, 0].astype(BF16) for r in v_refs], axis=1)
    z_all = jnp.dot(qbd, kcat, preferred_element_type=F32)
    carry = carry_ref[...]
    ws = []
    for p in range(pp):
        z = z_all[:, p * LANES:(p + 1) * LANES] + bias
        ls = -_softplus(z)
        incl = _rev_cumsum_lanes(ls)
        ws.append(jnp.exp(z + incl + carry).astype(BF16))
        carry = carry + jnp.sum(ls, axis=1, keepdims=True)
    carry_ref[...] = carry
    w_all = jnp.concatenate(ws, axis=1)
    acc = acc_ref[...] + lax.dot_general(w_all, vcat, (((1,), (1,)), ((), ())), preferred_element_type=F32)
    acc_ref[...] = acc

    @pl.when(s == n_steps - 1)
    def _():
        rows = lax.broadcasted_iota(jnp.int32, acc.shape, 0)
        cols = lax.broadcasted_iota(jnp.int32, acc.shape, 1)
        own = jnp.where(rows % N_HEADS == cols // HEAD_DIM, acc, 0.0)
        o_ref[0] = jnp.sum(own.reshape(nrow // N_HEADS, N_HEADS, W_A), axis=1)


def _attn_sample(qbd, bias_rows, k_new, v_new, kt_pool, vt_pool, page_table, layer, pp=8):
    nb, nrow, _ = qbd.shape
    n_pages = page_table.shape[1]
    n_steps = n_pages // pp

    def page_spec(p):
        return pl.BlockSpec(
            (1, 1, W_A, PAGE),
            lambda b, s, pt: (layer, pt[b, n_pages - 1 - s * pp - p], 0, 0))

    grid_spec = pltpu.PrefetchScalarGridSpec(
        num_scalar_prefetch=1,
        grid=(nb, n_steps),
        in_specs=[pl.BlockSpec((1, nrow, W_A), lambda b, s, pt: (b, 0, 0)),
                  pl.BlockSpec((nrow, LANES), lambda b, s, pt: (0, 0)),
                  pl.BlockSpec((1, 8, W_A), lambda b, s, pt: (b, 0, 0)),
                  pl.BlockSpec((1, 8, W_A), lambda b, s, pt: (b, 0, 0))]
                 + [page_spec(p) for p in range(pp)] * 2,
        out_specs=pl.BlockSpec((1, nrow // N_HEADS, W_A), lambda b, s, pt: (b, 0, 0)),
        scratch_shapes=[pltpu.VMEM((nrow, W_A), F32), pltpu.VMEM((nrow, LANES), F32)],
    )
    return pl.pallas_call(
        functools.partial(_attn_sample_kernel, pp=pp, n_steps=n_steps),
        grid_spec=grid_spec,
        out_shape=jax.ShapeDtypeStruct((nb, nrow // N_HEADS, W_A), F32),
        compiler_params=_params(2, VMEM_LIMIT),
        name="attn_sample",
    )(page_table, qbd, bias_rows, k_new, v_new, *([kt_pool] * pp), *([vt_pool] * pp))


def _conv_kernel(glu_ref, hist_ref, w_ref, b_ref, lng_ref, lnb_ref, ob_ref, tail_ref, xc_ref,
                 *, step, hp, tm, carry_tiles):
    @pl.when(pl.program_id(1) == 0)
    def _():
        xc_ref[0:hp] = hist_ref[0]

    xc_ref[hp:hp + tm] = glu_ref[0]
    acc = jnp.broadcast_to(b_ref[...], (tm, W_B))
    base = hp - (CONV_W - 1) * step
    for k in range(CONV_W):
        acc = acc + w_ref[k:k + 1, :] * xc_ref[base + k * step:base + k * step + tm, :]
    y = _layer_norm(acc, lng_ref[...], lnb_ref[...])
    ob_ref[0] = y * _sigmoid(y)
    tail = xc_ref[tm:tm + hp]
    tail_ref[0] = tail
    if carry_tiles:
        xc_ref[0:hp] = tail


def _conv(glu, hist, w, b, lng, lnb, step, tm):
    bx, tx, _ = glu.shape
    hp = hist.shape[1]
    return pl.pallas_call(
        functools.partial(_conv_kernel, step=step, hp=hp, tm=tm, carry_tiles=tx > tm),
        grid=(bx, tx // tm),
        in_specs=[pl.BlockSpec((1, tm, W_B), lambda b, i: (b, i, 0)),
                  pl.BlockSpec((1, hp, W_B), lambda b, i: (b, 0, 0)),
                  _full((32, W_B)), _full((1, W_B)), _full((1, W_B)), _full((1, W_B))],
        out_specs=[pl.BlockSpec((1, tm, W_B), lambda b, i: (b, i, 0)),
                   pl.BlockSpec((1, hp, W_B), lambda b, i: (b, 0, 0))],
        out_shape=[jax.ShapeDtypeStruct((bx, tx, W_B), F32), jax.ShapeDtypeStruct((bx, hp, W_B), F32)],
        scratch_shapes=[pltpu.VMEM((hp + tm, W_B), F32)],
        compiler_params=_params(2, VMEM_LIMIT),
        name="conv",
    )(glu, hist, w, b, lng, lnb)


def _mix_kernel(oa_ref, ob_ref, gu_ref, vsg_ref, wm_ref, bm_ref, wo_ref, x_ref, g1_ref, o_ref, *, tm):
    lane = lax.broadcasted_iota(jnp.int32, (CHUNK, W_C), 1)
    parts = []
    for c in range(tm // CHUNK):
        rows = slice(c * CHUNK, (c + 1) * CHUNK)
        vs = vsg_ref[0, rows, :].astype(BF16)
        sp = bm_ref[...]
        for g in range(N_GROUPS_C):
            r = jnp.dot(wm_ref[g], vs, preferred_element_type=F32)
            sp = sp + jnp.where(lane // HEAD_DIM == g, r, 0.0)
        parts.append(gu_ref[0, rows, :] * sp)
    oc = jnp.concatenate(parts, axis=0) if len(parts) > 1 else parts[0]
    cat = jnp.concatenate([oa_ref[0], ob_ref[0], oc], axis=1).astype(BF16)
    y = jnp.dot(cat, wo_ref[...], preferred_element_type=F32)
    o_ref[0] = x_ref[0] + g1_ref[0] * y


def _mix(oa, ob, gu, vsg, wm, bm, wo, x, g1, tm):
    bx, tx, _ = x.shape
    rm = g1.shape[1]
    mod_spec = (pl.BlockSpec((1, 1, D_MODEL), lambda b, i: (b, 0, 0)) if rm == 1 else
                pl.BlockSpec((1, tm, D_MODEL), lambda b, i: (b, i, 0)))

    def row(w):
        return pl.BlockSpec((1, tm, w), lambda b, i: (b, i, 0))

    return pl.pallas_call(
        functools.partial(_mix_kernel, tm=tm),
        grid=(bx, tx // tm),
        in_specs=[row(W_A), row(W_B), row(W_C), row(W_C), _full((N_GROUPS_C, CHUNK, CHUNK)),
                  _full((CHUNK, W_C)), _full((D_MODEL, D_MODEL)), row(D_MODEL), mod_spec],
        out_specs=row(D_MODEL),
        out_shape=jax.ShapeDtypeStruct((bx, tx, D_MODEL), F32),
        compiler_params=_params(2, VMEM_LIMIT),
        name="mix",
    )(oa, ob, gu, vsg, wm, bm, wo, x, g1)


def _ffn_kernel(x_ref, sh_ref, sc_ref, g2_ref, ng_ref, wg_ref, wu_ref, wd_ref, cw_ref, cb_ref, hist_ref,
                o_ref, tail_ref, xc_ref, *, step, hp, tm, carry_tiles):
    @pl.when(pl.program_id(1) == 0)
    def _():
        xc_ref[0:hp] = hist_ref[0]

    x = x_ref[0]
    y = x * lax.rsqrt(jnp.mean(x * x, axis=-1, keepdims=True) + EPS) * ng_ref[...]
    h = (y * (1.0 + sc_ref[0]) + sh_ref[0]).astype(BF16)
    gate = jnp.dot(h, wg_ref[...], preferred_element_type=F32)
    xc_ref[hp:hp + tm] = gate
    up = jnp.dot(h, wu_ref[...], preferred_element_type=F32)
    gc = (cw_ref[0:1, :] * xc_ref[hp - 2 * step:hp - 2 * step + tm, :]
          + cw_ref[1:2, :] * xc_ref[hp - step:hp - step + tm, :]
          + cw_ref[2:3, :] * gate + cb_ref[...])
    act = (gc * _sigmoid(gc) * up).astype(BF16)
    o_ref[0] = x + g2_ref[0] * jnp.dot(act, wd_ref[...], preferred_element_type=F32)
    tail = xc_ref[tm:tm + hp]
    tail_ref[0] = tail
    if carry_tiles:
        xc_ref[0:hp] = tail


def _ffn(x, sh, sc, g2, ng, wg, wu, wd, cw, cb, hist, step, tm):
    bx, tx, _ = x.shape
    hp = hist.shape[1]
    rm = sh.shape[1]
    mod_spec = (pl.BlockSpec((1, 1, D_MODEL), lambda b, i: (b, 0, 0)) if rm == 1 else
                pl.BlockSpec((1, tm, D_MODEL), lambda b, i: (b, i, 0)))
    row = pl.BlockSpec((1, tm, D_MODEL), lambda b, i: (b, i, 0))
    return pl.pallas_call(
        functools.partial(_ffn_kernel, step=step, hp=hp, tm=tm, carry_tiles=tx > tm),
        grid=(bx, tx // tm),
        in_specs=[row, mod_spec, mod_spec, mod_spec, _full((1, D_MODEL)),
                  _full((D_MODEL, D_FF)), _full((D_MODEL, D_FF)), _full((D_FF, D_MODEL)),
                  _full((8, D_FF)), _full((1, D_FF)),
                  pl.BlockSpec((1, hp, D_FF), lambda b, i: (b, 0, 0))],
        out_specs=[row, pl.BlockSpec((1, hp, D_FF), lambda b, i: (b, 0, 0))],
        out_shape=[jax.ShapeDtypeStruct((bx, tx, D_MODEL), F32), jax.ShapeDtypeStruct((bx, hp, D_FF), F32)],
        scratch_shapes=[pltpu.VMEM((hp + tm, D_FF), F32)],
        compiler_params=_params(2, VMEM_LIMIT),
        name="ffn",
    )(x, sh, sc, g2, ng, wg, wu, wd, cw, cb, hist)


def kernel(x_prompt, x_sample, cache_k, cache_v, state_conv, state_ffn, page_table, c_prompt, c_sample,
           ada_w, ada_b, norm1_g, norm2_g, w_in, q_norm_g, k_norm_g, sb_bias, conv_w, conv_b, conv_ln_g,
           conv_ln_b, sg_ln_g, sg_ln_b, sg_w, sg_b, w_o, ffn_wg, ffn_wu, ffn_conv_w, ffn_conv_b, ffn_wd):
    nb, seq, _ = x_prompt.shape
    db, dt, _ = x_sample.shape
    n_rows = db * dt
    n_pool = cache_k.shape[1]

    c_all = jnp.concatenate([c_prompt, c_sample], axis=0)
    c_pad = (-c_all.shape[0]) % 8
    c_all = jnp.pad(c_all, ((0, c_pad), (0, 0)))
    mod = _modulation(c_all, ada_w, ada_b)

    r = jnp.arange(2 * QBLOCK)
    later = (r[:, None] % QBLOCK) > r[None, :]
    ss = jnp.where(r[None, :] < QBLOCK, later, True).astype(BF16)
    h_of = jnp.arange(W_A) // HEAD_DIM
    gmat = jnp.where(h_of[:, None] == h_of[None, :], 1.0 / HEAD_DIM, 0.0).astype(BF16)
    eye_h = jnp.eye(N_HEADS, dtype=bool)
    tri = jnp.tril(jnp.ones((CHUNK, CHUNK), bool))

    kt_pool = jnp.transpose(cache_k, (0, 1, 3, 4, 2)).reshape(DEPTH, n_pool, W_A, PAGE)
    vt_pool = jnp.transpose(cache_v, (0, 1, 3, 4, 2)).reshape(DEPTH, n_pool, W_A, PAGE)

    xp = x_prompt
    xs = jnp.transpose(x_sample, (1, 0, 2)).reshape(1, n_rows, D_MODEL)
    outs = {n: [] for n in ("kp", "vp", "ks", "vs", "cp", "cs", "fp", "fs", "gs")}
    zeros_conv = jnp.zeros((nb, 32, W_B), F32)
    zeros_ffn = jnp.zeros((nb, 8, D_FF), F32)

    def to_seq(a):
        return jnp.transpose(a.reshape(dt, db, a.shape[-1]), (1, 0, 2))

    def to_tm(a):
        return jnp.transpose(a, (1, 0, 2)).reshape(1, n_rows, a.shape[-1])

    for l in range(DEPTH):
        mod_p = [m[:, None, :] for m in jnp.split(mod[l, :nb], 6, axis=-1)]
        mod_s = [jnp.tile(m, (dt, 1))[None] for m in jnp.split(mod[l, nb:nb + db], 6, axis=-1)]
        w_in_b = w_in[l].astype(BF16)
        wo_b = w_o[l].astype(BF16)
        wg_b, wu_b, wd_b = ffn_wg[l].astype(BF16), ffn_wu[l].astype(BF16), ffn_wd[l].astype(BF16)
        qg = jnp.tile(q_norm_g[l], N_HEADS)[None]
        kg = jnp.tile(k_norm_g[l], N_HEADS)[None]
        n1, n2 = norm1_g[l][None], norm2_g[l][None]
        lng, lnb = sg_ln_g[l][None], sg_ln_b[l][None]
        cw = jnp.pad(conv_w[l], ((0, 1), (0, 0)))
        cb, clg, clb = conv_b[l][None], conv_ln_g[l][None], conv_ln_b[l][None]
        fcw = jnp.pad(ffn_conv_w[l], ((0, 5), (0, 0)))
        fcb = ffn_conv_b[l][None]

        sh1, sc1, g1, sh2, sc2, g2 = mod_p
        q, kf, vf, kb, vb, glu, gu, vsg = _pre(xp, sh1, sc1, n1, w_in_b, qg, kg, gmat, lng, lnb, tm=512)
        oa = _attn_prompt(q, kb, vb, sb_bias[l], ss)
        ob, ctail = _conv(glu, zeros_conv, cw, cb, clg, clb, step=1, tm=512)
        wm = jnp.where(tri[None], sg_w[l], 0.0).astype(BF16)
        bm = jnp.repeat(sg_b[l].T, HEAD_DIM, axis=1)
        xp = _mix(oa, ob, gu, vsg, wm, bm, wo_b, xp, g1, tm=512)
        xp, ftail = _ffn(xp, sh2, sc2, g2, n2, wg_b, wu_b, wd_b, fcw, fcb, zeros_ffn, step=1, tm=256)
        outs["kp"].append(kf.reshape(nb, seq, N_HEADS, HEAD_DIM))
        outs["vp"].append(vf.reshape(nb, seq, N_HEADS, HEAD_DIM))
        outs["cp"].append(ctail[:, 32 - (CONV_W - 1):])
        outs["fp"].append(ftail[:, 8 - (FFN_CONV_W - 1):])

        sh1, sc1, g1, sh2, sc2, g2 = mod_s
        q, kf, vf, kb, vb, glu, gu, vsg = _pre(xs, sh1, sc1, n1, w_in_b, qg, kg, gmat, lng, lnb, tm=n_rows)
        q_seq = to_seq(q).reshape(db, dt, N_HEADS, HEAD_DIM)
        qbd = jnp.where(eye_h[None, None, :, :, None], q_seq[:, :, None, :, :],
                        jnp.zeros((), BF16)).reshape(db, dt * N_HEADS, W_A)
        bias_rows = jnp.broadcast_to(jnp.tile(sb_bias[l], dt)[:, None], (dt * N_HEADS, LANES))
        k_new = jnp.pad(to_seq(kb).astype(F32), ((0, 0), (0, 8 - dt), (0, 0)))
        v_new = jnp.pad(to_seq(vb).astype(F32), ((0, 0), (0, 8 - dt), (0, 0)))
        oa = to_tm(_attn_sample(qbd, bias_rows, k_new, v_new, kt_pool, vt_pool, page_table, l))
        hist = jnp.transpose(state_conv[l], (1, 0, 2)).reshape(1, (CONV_W - 1) * db, W_B)
        ob, ctail = _conv(glu, hist, cw, cb, clg, clb, step=db, tm=n_rows)
        w4 = jnp.where(tri[:dt, :dt][None], sg_w[l][:, :dt, :dt], 0.0)
        wm = jnp.einsum("gts,bc->gtbsc", w4, jnp.eye(db, dtype=F32)).reshape(N_GROUPS_C, n_rows, n_rows)
        bm = jnp.repeat(jnp.repeat(sg_b[l][:, :dt].T, db, axis=0), HEAD_DIM, axis=1)
        xs = _mix(oa, ob, gu, vsg, wm.astype(BF16), bm, wo_b, xs, g1, tm=n_rows)
        fhist = jnp.transpose(state_ffn[l], (1, 0, 2)).reshape(1, (FFN_CONV_W - 1) * db, D_FF)
        xs, ftail = _ffn(xs, sh2, sc2, g2, n2, wg_b, wu_b, wd_b, fcw, fcb, fhist, step=db, tm=n_rows)
        outs["ks"].append(to_seq(kf).reshape(db, dt, N_HEADS, HEAD_DIM))
        outs["vs"].append(to_seq(vf).reshape(db, dt, N_HEADS, HEAD_DIM))
        outs["cs"].append(jnp.transpose(ctail.reshape(CONV_W - 1, db, W_B), (1, 0, 2)))
        outs["fs"].append(jnp.transpose(ftail.reshape(FFN_CONV_W - 1, db, D_FF), (1, 0, 2)))
        outs["gs"].append(to_seq(vsg))

    ys = to_seq(xs)
    st = {n: jnp.stack(v) for n, v in outs.items()}
    return (xp, ys, st["kp"], st["vp"], st["ks"], st["vs"], st["cp"], st["cs"], st["fp"], st["fs"], st["gs"])
```

```python
import functools

import jax
import jax.numpy as jnp
from jax import lax
from jax.experimental import pallas as pl
from jax.experimental.pallas import tpu as pltpu

F32 = jnp.float32
BF16 = jnp.bfloat16

D_MODEL = 1024
DEPTH = 2
HEAD_DIM = 64
W_A = 512
N_HEADS = 8
W_B = 256
W_C = 256
N_GROUPS_C = 4
CONV_W = 31
CHUNK = 128
QBLOCK = 128
FFN_CONV_W = 3
D_FF = 2816
IN_WIDTH = 3 * W_A + 2 * W_B + 2 * W_C
EPS = 1e-6
PAGE = 128

MASKED_LOG_WEIGHT = -1e30
PROMPT_GK = 8
SAMPLE_PAGES_PER_STEP = 16

LANES = 128
VMEM_LIMIT = 56 * 1024 * 1024


def _params(n_axes, vmem=None):
    return pltpu.CompilerParams(dimension_semantics=("arbitrary",) * n_axes,
                                vmem_limit_bytes=vmem)


def _full(shape):
    return pl.BlockSpec(shape, lambda *_: (0,) * len(shape))


def _sigmoid(x):
    return 1.0 / (1.0 + jnp.exp(-x))


def _gelu_tanh(x):
    c = 0.7978845608028654
    return x * (0.5 * (1.0 + jnp.tanh(c * (x + 0.044715 * (x * x * x)))))


def _layer_norm(x, g, b):
    xc = x - jnp.mean(x, axis=-1, keepdims=True)
    return xc * lax.rsqrt(jnp.mean(xc * xc, axis=-1, keepdims=True) + EPS) * g + b


def _softplus(z):
    neg_abs = lax.bitcast_convert_type(
        lax.bitcast_convert_type(z, jnp.uint32) | jnp.uint32(0x80000000), F32)
    return jnp.maximum(z, 0.0) + jnp.log(1.0 + jnp.exp(neg_abs))


def _neg_suffix_matrix():
    r = jnp.arange(2 * LANES)
    later = (r[:, None] % LANES) > r[None, :]
    return -jnp.where(r[None, :] < LANES, later, True).astype(BF16)


def _stick_block(z, neg_ss, mask=None):
    sp = _softplus(z)
    if mask is not None:
        sp = jnp.where(mask, sp, 0.0)
    hi = sp.astype(BF16)
    lo = (sp - hi.astype(F32)).astype(BF16)
    res = jnp.dot(jnp.concatenate([hi, lo], axis=1), neg_ss, preferred_element_type=F32)
    return z - sp + res[:, :LANES], res[:, LANES:]


def _stick_weights(z, carry, neg_ss, mask=None):
    arg, tot = _stick_block(z, neg_ss, mask)
    w = jnp.exp(arg + carry)
    if mask is not None:
        w = jnp.where(mask, w, 0.0)
    return w.astype(BF16), carry + tot


def _mod_kernel(c_ref, w_ref, b_ref, o_ref):
    c = c_ref[...]
    s = (c * _sigmoid(c)).astype(BF16)
    o_ref[0] = jnp.dot(s, w_ref[0].astype(BF16), preferred_element_type=F32) + b_ref[0]


def _modulation(c_all, ada_w, ada_b):
    rows = c_all.shape[0]
    tn = 1536
    return pl.pallas_call(
        _mod_kernel,
        grid=(DEPTH, 6 * D_MODEL // tn),
        in_specs=[_full((rows, D_MODEL)),
                  pl.BlockSpec((1, D_MODEL, tn), lambda l, n: (l, 0, n)),
                  pl.BlockSpec((1, 1, tn), lambda l, n: (l, 0, n))],
        out_specs=pl.BlockSpec((1, rows, tn), lambda l, n: (l, 0, n)),
        out_shape=jax.ShapeDtypeStruct((DEPTH, rows, 6 * D_MODEL), F32),
        compiler_params=_params(2, VMEM_LIMIT),
        name="modulation",
    )(c_all, ada_w, ada_b.reshape(DEPTH, 1, 6 * D_MODEL))


def _pre_kernel(x_ref, sh_ref, sc_ref, g_ref, win_ref, qg_ref, kg_ref, gmat_ref, lng_ref, lnb_ref,
                q_ref, kf_ref, vf_ref, kb_ref, vb_ref, glu_ref, gu_ref, vsg_ref):
    x = x_ref[0]
    y = x * lax.rsqrt(jnp.mean(x * x, axis=-1, keepdims=True) + EPS) * g_ref[...]
    h = (y * (1.0 + sc_ref[0]) + sh_ref[0]).astype(BF16)

    def proj(lo, hi):
        return jnp.dot(h, win_ref[:, lo:hi], preferred_element_type=F32)

    def head_norm(t, g):
        ms = jnp.dot((t * t).astype(BF16), gmat_ref[...], preferred_element_type=F32)
        return t * lax.rsqrt(ms + EPS) * g

    q = head_norm(proj(0, W_A), qg_ref[...])
    q_ref[0] = (q * (HEAD_DIM ** -0.5)).astype(BF16)
    k = head_norm(proj(W_A, 2 * W_A), kg_ref[...])
    kf_ref[0] = k
    kb_ref[0] = k.astype(BF16)
    v = proj(2 * W_A, 3 * W_A)
    vf_ref[0] = v
    vb_ref[0] = v.astype(BF16)
    o = 3 * W_A
    glu_ref[0] = proj(o, o + W_B) * _sigmoid(proj(o + W_B, o + 2 * W_B))
    o += 2 * W_B
    gu_ref[0] = _gelu_tanh(proj(o, o + W_C))
    vsg_ref[0] = _layer_norm(_gelu_tanh(proj(o + W_C, o + 2 * W_C)), lng_ref[...], lnb_ref[...])


def _pre(x, sh, sc, norm_g, w_in, qg, kg, gmat, lng, lnb, tm):
    bx, tx, _ = x.shape
    rm = sh.shape[1]
    mod_spec = (pl.BlockSpec((1, 1, D_MODEL), lambda b, i: (b, 0, 0)) if rm == 1 else
                pl.BlockSpec((1, tm, D_MODEL), lambda b, i: (b, i, 0)))

    def row(w):
        return pl.BlockSpec((1, tm, w), lambda b, i: (b, i, 0))

    widths = (W_A, W_A, W_A, W_A, W_A, W_B, W_C, W_C)
    dtypes = (BF16, F32, F32, BF16, BF16, F32, F32, F32)
    return pl.pallas_call(
        _pre_kernel,
        grid=(bx, tx // tm),
        in_specs=[row(D_MODEL), mod_spec, mod_spec, _full((1, D_MODEL)), _full((D_MODEL, IN_WIDTH)),
                  _full((1, W_A)), _full((1, W_A)), _full((W_A, W_A)), _full((1, W_C)), _full((1, W_C))],
        out_specs=[row(w) for w in widths],
        out_shape=[jax.ShapeDtypeStruct((bx, tx, w), d) for w, d in zip(widths, dtypes)],
        compiler_params=_params(2, VMEM_LIMIT),
        name="pre",
    )(x, sh, sc, norm_g, w_in, qg, kg, gmat, lng, lnb)


def _attn_prompt_kernel(bias_ref, q_ref, kt_ref, v_ref, ss_ref, o_ref, arg_ref, tot_ref, carry_ref, acc_ref,
                        *, gk):
    p = pl.program_id(1)
    i = pl.program_id(2)
    gkeys = gk * QBLOCK
    q = q_ref[0].astype(F32)
    first = lax.broadcasted_iota(jnp.int32, (QBLOCK, LANES), 1) < HEAD_DIM
    qs = (jnp.where(first, q, 0.0).astype(BF16), jnp.where(first, 0.0, q).astype(BF16))
    biases = (bias_ref[2 * p], bias_ref[2 * p + 1])
    neg_ss = ss_ref[...]
    g_diag = i // gk

    def scores(g):
        kt = kt_ref[0, 0, g]
        return tuple(jnp.dot(qs[h], kt, preferred_element_type=F32) + biases[h] for h in range(2))

    def logits(zs, mask):
        run = [None, None]
        for blk in reversed(range(gk)):
            cols = slice(blk * QBLOCK, (blk + 1) * QBLOCK)
            m = None if mask is None else mask[:, cols]
            for h in range(2):
                a, tot = _stick_block(zs[h][:, cols], neg_ss, m)
                if m is not None:
                    a = jnp.where(m, a, MASKED_LOG_WEIGHT)
                arg_ref[h, :, cols] = a if run[h] is None else a + run[h]
                run[h] = tot if run[h] is None else run[h] + tot
        for h in range(2):
            tot_ref[h] = run[h]

    def weigh(g):
        v = v_ref[0, pl.ds(pl.multiple_of(g * gkeys, gkeys), gkeys), :]
        for h in range(2):
            carry = carry_ref[h]
            w = jnp.exp(arg_ref[h] + jnp.concatenate([carry] * gk, axis=1)).astype(BF16)
            acc_ref[h] += jnp.dot(w, v, preferred_element_type=F32)
            carry_ref[h] = carry + tot_ref[h]

    kpos = g_diag * gkeys + lax.broadcasted_iota(jnp.int32, (QBLOCK, gkeys), 1)
    qpos = i * QBLOCK + lax.broadcasted_iota(jnp.int32, (QBLOCK, gkeys), 0)
    carry_ref[...] = jnp.zeros_like(carry_ref)
    acc_ref[...] = jnp.zeros_like(acc_ref)
    logits(scores(g_diag), kpos < qpos)

    @pl.loop(0, g_diag)
    def _(it):
        g = g_diag - 1 - it
        zs = scores(g)
        weigh(g + 1)
        logits(zs, None)

    weigh(0)
    o_ref[0] = jnp.where(first, acc_ref[0], acc_ref[1])


def _attn_prompt(q, kt, v, bias, neg_ss, gk):
    b, t, _ = q.shape
    n_groups = t // (gk * QBLOCK)
    grid_spec = pltpu.PrefetchScalarGridSpec(
        num_scalar_prefetch=1,
        grid=(b, W_A // LANES, t // QBLOCK),
        in_specs=[pl.BlockSpec((1, QBLOCK, LANES), lambda b, p, i, s: (b, i, p)),
                  pl.BlockSpec((1, 1, n_groups, LANES, gk * QBLOCK), lambda b, p, i, s: (b, p, 0, 0, 0)),
                  pl.BlockSpec((1, t, LANES), lambda b, p, i, s: (b, 0, p)),
                  pl.BlockSpec((2 * LANES, 2 * LANES), lambda b, p, i, s: (0, 0))],
        out_specs=pl.BlockSpec((1, QBLOCK, LANES), lambda b, p, i, s: (b, i, p)),
        scratch_shapes=[pltpu.VMEM((2, QBLOCK, gk * QBLOCK), F32)] + [pltpu.VMEM((2, QBLOCK, LANES), F32)] * 3,
    )
    return pl.pallas_call(
        functools.partial(_attn_prompt_kernel, gk=gk),
        grid_spec=grid_spec,
        out_shape=jax.ShapeDtypeStruct((b, t, W_A), F32),
        compiler_params=_params(3, VMEM_LIMIT),
        name="attn_prompt",
    )(bias, q, kt, v, neg_ss)


def _attn_sample_kernel(pt_ref, qbd_ref, bias_ref, kn_ref, vn_ref, ss_ref, *refs, pp, n_steps):
    del pt_ref
    k_refs, v_refs = refs[:pp], refs[pp:2 * pp]
    o_ref, acc_ref, carry_ref = refs[2 * pp:]
    s = pl.program_id(1)
    qbd = qbd_ref[0]
    bias = bias_ref[...]
    neg_ss = ss_ref[...]
    nrow = qbd.shape[0]

    @pl.when(s == 0)
    def _():
        lane = lax.broadcasted_iota(jnp.int32, (nrow, LANES), 1)
        row = lax.broadcasted_iota(jnp.int32, (nrow, LANES), 0)
        pad = jnp.zeros((PAGE - 8, W_A), F32)
        kn = jnp.concatenate([kn_ref[0], pad], axis=0).astype(BF16)
        vn = jnp.concatenate([vn_ref[0], pad], axis=0).astype(BF16)
        z = lax.dot_general(qbd, kn, (((1,), (1,)), ((), ())), preferred_element_type=F32) + bias
        w, carry0 = _stick_weights(z, jnp.zeros((nrow, LANES), F32), neg_ss, lane < row // N_HEADS)
        acc_ref[...] = jnp.dot(w, vn, preferred_element_type=F32)
        carry_ref[...] = carry0

    kcat = jnp.concatenate([r[0, 0].astype(BF16) for r in k_refs], axis=1)
    vcat = jnp.concatenate([r[0, 0].astype(BF16) for r in v_refs], axis=1)
    z_all = jnp.dot(qbd, kcat, preferred_element_type=F32)
    carry = carry_ref[...]
    ws = []
    for p in range(pp):
        w, carry = _stick_weights(z_all[:, p * LANES:(p + 1) * LANES] + bias, carry, neg_ss)
        ws.append(w)
    carry_ref[...] = carry
    w_all = jnp.concatenate(ws, axis=1)
    acc = acc_ref[...] + lax.dot_general(w_all, vcat, (((1,), (1,)), ((), ())), preferred_element_type=F32)
    acc_ref[...] = acc

    @pl.when(s == n_steps - 1)
    def _():
        rows = lax.broadcasted_iota(jnp.int32, acc.shape, 0)
        cols = lax.broadcasted_iota(jnp.int32, acc.shape, 1)
        own = jnp.where(rows % N_HEADS == cols // HEAD_DIM, acc, 0.0)
        o_ref[0] = jnp.sum(own.reshape(nrow // N_HEADS, N_HEADS, W_A), axis=1)


def _attn_sample(qbd, bias_rows, k_new, v_new, neg_ss, kt_pool, vt_pool, page_table, layer, pp):
    nb, nrow, _ = qbd.shape
    n_pages = page_table.shape[1]
    n_steps = n_pages // pp

    def page_spec(p):
        return pl.BlockSpec(
            (1, 1, W_A, PAGE),
            lambda b, s, pt: (layer, pt[b, n_pages - 1 - s * pp - p], 0, 0))

    grid_spec = pltpu.PrefetchScalarGridSpec(
        num_scalar_prefetch=1,
        grid=(nb, n_steps),
        in_specs=[pl.BlockSpec((1, nrow, W_A), lambda b, s, pt: (b, 0, 0)),
                  pl.BlockSpec((nrow, LANES), lambda b, s, pt: (0, 0)),
                  pl.BlockSpec((1, 8, W_A), lambda b, s, pt: (b, 0, 0)),
                  pl.BlockSpec((1, 8, W_A), lambda b, s, pt: (b, 0, 0)),
                  pl.BlockSpec((2 * LANES, 2 * LANES), lambda b, s, pt: (0, 0))]
                 + [page_spec(p) for p in range(pp)] * 2,
        out_specs=pl.BlockSpec((1, nrow // N_HEADS, W_A), lambda b, s, pt: (b, 0, 0)),
        scratch_shapes=[pltpu.VMEM((nrow, W_A), F32), pltpu.VMEM((nrow, LANES), F32)],
    )
    return pl.pallas_call(
        functools.partial(_attn_sample_kernel, pp=pp, n_steps=n_steps),
        grid_spec=grid_spec,
        out_shape=jax.ShapeDtypeStruct((nb, nrow // N_HEADS, W_A), F32),
        compiler_params=_params(2, VMEM_LIMIT),
        name="attn_sample",
    )(page_table, qbd, bias_rows, k_new, v_new, neg_ss, *([kt_pool] * pp), *([vt_pool] * pp))


def _conv_kernel(glu_ref, hist_ref, w_ref, b_ref, lng_ref, lnb_ref, ob_ref, tail_ref, xc_ref,
                 *, step, hp, tm, carry_tiles):
    @pl.when(pl.program_id(1) == 0)
    def _():
        xc_ref[0:hp] = hist_ref[0]

    xc_ref[hp:hp + tm] = glu_ref[0]
    acc = jnp.broadcast_to(b_ref[...], (tm, W_B))
    base = hp - (CONV_W - 1) * step
    for k in range(CONV_W):
        acc = acc + w_ref[k:k + 1, :] * xc_ref[base + k * step:base + k * step + tm, :]
    y = _layer_norm(acc, lng_ref[...], lnb_ref[...])
    ob_ref[0] = y * _sigmoid(y)
    tail = xc_ref[tm:tm + hp]
    tail_ref[0] = tail
    if carry_tiles:
        xc_ref[0:hp] = tail


def _conv(glu, hist, w, b, lng, lnb, step, tm):
    bx, tx, _ = glu.shape
    hp = hist.shape[1]
    return pl.pallas_call(
        functools.partial(_conv_kernel, step=step, hp=hp, tm=tm, carry_tiles=tx > tm),
        grid=(bx, tx // tm),
        in_specs=[pl.BlockSpec((1, tm, W_B), lambda b, i: (b, i, 0)),
                  pl.BlockSpec((1, hp, W_B), lambda b, i: (b, 0, 0)),
                  _full((32, W_B)), _full((1, W_B)), _full((1, W_B)), _full((1, W_B))],
        out_specs=[pl.BlockSpec((1, tm, W_B), lambda b, i: (b, i, 0)),
                   pl.BlockSpec((1, hp, W_B), lambda b, i: (b, 0, 0))],
        out_shape=[jax.ShapeDtypeStruct((bx, tx, W_B), F32), jax.ShapeDtypeStruct((bx, hp, W_B), F32)],
        scratch_shapes=[pltpu.VMEM((hp + tm, W_B), F32)],
        compiler_params=_params(2, VMEM_LIMIT),
        name="conv",
    )(glu, hist, w, b, lng, lnb)


def _mix_kernel(oa_ref, ob_ref, gu_ref, vsg_ref, wm_ref, bm_ref, wo_ref, x_ref, g1_ref, o_ref, *, tm):
    lane = lax.broadcasted_iota(jnp.int32, (CHUNK, W_C), 1)
    parts = []
    for c in range(tm // CHUNK):
        rows = slice(c * CHUNK, (c + 1) * CHUNK)
        vs = vsg_ref[0, rows, :].astype(BF16)
        sp = bm_ref[...]
        for g in range(N_GROUPS_C):
            r = jnp.dot(wm_ref[g], vs, preferred_element_type=F32)
            sp = sp + jnp.where(lane // HEAD_DIM == g, r, 0.0)
        parts.append(gu_ref[0, rows, :] * sp)
    oc = jnp.concatenate(parts, axis=0) if len(parts) > 1 else parts[0]
    cat = jnp.concatenate([oa_ref[0], ob_ref[0], oc], axis=1).astype(BF16)
    y = jnp.dot(cat, wo_ref[...], preferred_element_type=F32)
    o_ref[0] = x_ref[0] + g1_ref[0] * y


def _mix(oa, ob, gu, vsg, wm, bm, wo, x, g1, tm):
    bx, tx, _ = x.shape
    rm = g1.shape[1]
    mod_spec = (pl.BlockSpec((1, 1, D_MODEL), lambda b, i: (b, 0, 0)) if rm == 1 else
                pl.BlockSpec((1, tm, D_MODEL), lambda b, i: (b, i, 0)))

    def row(w):
        return pl.BlockSpec((1, tm, w), lambda b, i: (b, i, 0))

    return pl.pallas_call(
        functools.partial(_mix_kernel, tm=tm),
        grid=(bx, tx // tm),
        in_specs=[row(W_A), row(W_B), row(W_C), row(W_C), _full((N_GROUPS_C, CHUNK, CHUNK)),
                  _full((CHUNK, W_C)), _full((D_MODEL, D_MODEL)), row(D_MODEL), mod_spec],
        out_specs=row(D_MODEL),
        out_shape=jax.ShapeDtypeStruct((bx, tx, D_MODEL), F32),
        compiler_params=_params(2, VMEM_LIMIT),
        name="mix",
    )(oa, ob, gu, vsg, wm, bm, wo, x, g1)


def _ffn_kernel(x_ref, sh_ref, sc_ref, g2_ref, ng_ref, wg_ref, wu_ref, wd_ref, cw_ref, cb_ref, hist_ref,
                o_ref, tail_ref, xc_ref, *, step, hp, tm, carry_tiles):
    @pl.when(pl.program_id(1) == 0)
    def _():
        xc_ref[0:hp] = hist_ref[0]

    x = x_ref[0]
    y = x * lax.rsqrt(jnp.mean(x * x, axis=-1, keepdims=True) + EPS) * ng_ref[...]
    h = (y * (1.0 + sc_ref[0]) + sh_ref[0]).astype(BF16)
    gate = jnp.dot(h, wg_ref[...], preferred_element_type=F32)
    xc_ref[hp:hp + tm] = gate
    up = jnp.dot(h, wu_ref[...], preferred_element_type=F32)
    gc = (cw_ref[0:1, :] * xc_ref[hp - 2 * step:hp - 2 * step + tm, :]
          + cw_ref[1:2, :] * xc_ref[hp - step:hp - step + tm, :]
          + cw_ref[2:3, :] * gate + cb_ref[...])
    act = (gc * _sigmoid(gc) * up).astype(BF16)
    o_ref[0] = x + g2_ref[0] * jnp.dot(act, wd_ref[...], preferred_element_type=F32)
    tail = xc_ref[tm:tm + hp]
    tail_ref[0] = tail
    if carry_tiles:
        xc_ref[0:hp] = tail


def _ffn(x, sh, sc, g2, ng, wg, wu, wd, cw, cb, hist, step, tm):
    bx, tx, _ = x.shape
    hp = hist.shape[1]
    rm = sh.shape[1]
    mod_spec = (pl.BlockSpec((1, 1, D_MODEL), lambda b, i: (b, 0, 0)) if rm == 1 else
                pl.BlockSpec((1, tm, D_MODEL), lambda b, i: (b, i, 0)))
    row = pl.BlockSpec((1, tm, D_MODEL), lambda b, i: (b, i, 0))
    return pl.pallas_call(
        functools.partial(_ffn_kernel, step=step, hp=hp, tm=tm, carry_tiles=tx > tm),
        grid=(bx, tx // tm),
        in_specs=[row, mod_spec, mod_spec, mod_spec, _full((1, D_MODEL)),
                  _full((D_MODEL, D_FF)), _full((D_MODEL, D_FF)), _full((D_FF, D_MODEL)),
                  _full((8, D_FF)), _full((1, D_FF)),
                  pl.BlockSpec((1, hp, D_FF), lambda b, i: (b, 0, 0))],
        out_specs=[row, pl.BlockSpec((1, hp, D_FF), lambda b, i: (b, 0, 0))],
        out_shape=[jax.ShapeDtypeStruct((bx, tx, D_MODEL), F32), jax.ShapeDtypeStruct((bx, hp, D_FF), F32)],
        scratch_shapes=[pltpu.VMEM((hp + tm, D_FF), F32)],
        compiler_params=_params(2, VMEM_LIMIT),
        name="ffn",
    )(x, sh, sc, g2, ng, wg, wu, wd, cw, cb, hist)


def kernel(x_prompt, x_sample, cache_k, cache_v, state_conv, state_ffn, page_table, c_prompt, c_sample,
           ada_w, ada_b, norm1_g, norm2_g, w_in, q_norm_g, k_norm_g, sb_bias, conv_w, conv_b, conv_ln_g,
           conv_ln_b, sg_ln_g, sg_ln_b, sg_w, sg_b, w_o, ffn_wg, ffn_wu, ffn_conv_w, ffn_conv_b, ffn_wd):
    nb, seq, _ = x_prompt.shape
    db, dt, _ = x_sample.shape
    n_rows = db * dt
    n_pool = cache_k.shape[1]

    c_all = jnp.concatenate([c_prompt, c_sample], axis=0)
    c_pad = (-c_all.shape[0]) % 8
    c_all = jnp.pad(c_all, ((0, c_pad), (0, 0)))
    mod = _modulation(c_all, ada_w, ada_b)

    neg_ss = _neg_suffix_matrix()
    h_of = jnp.arange(W_A) // HEAD_DIM
    gmat = jnp.where(h_of[:, None] == h_of[None, :], 1.0 / HEAD_DIM, 0.0).astype(BF16)
    eye_h = jnp.eye(N_HEADS, dtype=bool)
    tri = jnp.tril(jnp.ones((CHUNK, CHUNK), bool))

    kt_pool = jnp.transpose(cache_k, (0, 1, 3, 4, 2)).reshape(DEPTH, n_pool, W_A, PAGE)
    vt_pool = jnp.transpose(cache_v, (0, 1, 3, 4, 2)).reshape(DEPTH, n_pool, W_A, PAGE)

    xp = x_prompt
    xs = jnp.transpose(x_sample, (1, 0, 2)).reshape(1, n_rows, D_MODEL)
    outs = {n: [] for n in ("kp", "vp", "ks", "vs", "cp", "cs", "fp", "fs", "gs")}
    zeros_conv = jnp.zeros((nb, 32, W_B), F32)
    zeros_ffn = jnp.zeros((nb, 8, D_FF), F32)

    def to_seq(a):
        return jnp.transpose(a.reshape(dt, db, a.shape[-1]), (1, 0, 2))

    def to_tm(a):
        return jnp.transpose(a, (1, 0, 2)).reshape(1, n_rows, a.shape[-1])

    for l in range(DEPTH):
        mod_p = [m[:, None, :] for m in jnp.split(mod[l, :nb], 6, axis=-1)]
        mod_s = [jnp.tile(m, (dt, 1))[None] for m in jnp.split(mod[l, nb:nb + db], 6, axis=-1)]
        w_in_b = w_in[l].astype(BF16)
        wo_b = w_o[l].astype(BF16)
        wg_b, wu_b, wd_b = ffn_wg[l].astype(BF16), ffn_wu[l].astype(BF16), ffn_wd[l].astype(BF16)
        qg = jnp.tile(q_norm_g[l], N_HEADS)[None]
        kg = jnp.tile(k_norm_g[l], N_HEADS)[None]
        n1, n2 = norm1_g[l][None], norm2_g[l][None]
        lng, lnb = sg_ln_g[l][None], sg_ln_b[l][None]
        cw = jnp.pad(conv_w[l], ((0, 1), (0, 0)))
        cb, clg, clb = conv_b[l][None], conv_ln_g[l][None], conv_ln_b[l][None]
        fcw = jnp.pad(ffn_conv_w[l], ((0, 5), (0, 0)))
        fcb = ffn_conv_b[l][None]

        sh1, sc1, g1, sh2, sc2, g2 = mod_p
        q, kf, vf, kb, vb, glu, gu, vsg = _pre(xp, sh1, sc1, n1, w_in_b, qg, kg, gmat, lng, lnb, tm=512)
        gkeys = PROMPT_GK * QBLOCK
        kt = jnp.transpose(kb.reshape(nb, seq // gkeys, gkeys, W_A // LANES, LANES), (0, 3, 1, 4, 2))
        oa = _attn_prompt(q, kt, vb, sb_bias[l], neg_ss, PROMPT_GK)
        ob, ctail = _conv(glu, zeros_conv, cw, cb, clg, clb, step=1, tm=512)
        wm = jnp.where(tri[None], sg_w[l], 0.0).astype(BF16)
        bm = jnp.repeat(sg_b[l].T, HEAD_DIM, axis=1)
        xp = _mix(oa, ob, gu, vsg, wm, bm, wo_b, xp, g1, tm=512)
        xp, ftail = _ffn(xp, sh2, sc2, g2, n2, wg_b, wu_b, wd_b, fcw, fcb, zeros_ffn, step=1, tm=256)
        outs["kp"].append(kf.reshape(nb, seq, N_HEADS, HEAD_DIM))
        outs["vp"].append(vf.reshape(nb, seq, N_HEADS, HEAD_DIM))
        outs["cp"].append(ctail[:, 32 - (CONV_W - 1):])
        outs["fp"].append(ftail[:, 8 - (FFN_CONV_W - 1):])

        sh1, sc1, g1, sh2, sc2, g2 = mod_s
        q, kf, vf, kb, vb, glu, gu, vsg = _pre(xs, sh1, sc1, n1, w_in_b, qg, kg, gmat, lng, lnb, tm=n_rows)
        q_seq = to_seq(q).reshape(db, dt, N_HEADS, HEAD_DIM)
        qbd = jnp.where(eye_h[None, None, :, :, None], q_seq[:, :, None, :, :],
                        jnp.zeros((), BF16)).reshape(db, dt * N_HEADS, W_A)
        bias_rows = jnp.broadcast_to(jnp.tile(sb_bias[l], dt)[:, None], (dt * N_HEADS, LANES))
        k_new = jnp.pad(to_seq(kb).astype(F32), ((0, 0), (0, 8 - dt), (0, 0)))
        v_new = jnp.pad(to_seq(vb).astype(F32), ((0, 0), (0, 8 - dt), (0, 0)))
        oa = to_tm(_attn_sample(qbd, bias_rows, k_new, v_new, neg_ss, kt_pool, vt_pool, page_table, l,
                                SAMPLE_PAGES_PER_STEP))
        hist = jnp.transpose(state_conv[l], (1, 0, 2)).reshape(1, (CONV_W - 1) * db, W_B)
        ob, ctail = _conv(glu, hist, cw, cb, clg, clb, step=db, tm=n_rows)
        w4 = jnp.where(tri[:dt, :dt][None], sg_w[l][:, :dt, :dt], 0.0)
        wm = jnp.einsum("gts,bc->gtbsc", w4, jnp.eye(db, dtype=F32)).reshape(N_GROUPS_C, n_rows, n_rows)
        bm = jnp.repeat(jnp.repeat(sg_b[l][:, :dt].T, db, axis=0), HEAD_DIM, axis=1)
        xs = _mix(oa, ob, gu, vsg, wm.astype(BF16), bm, wo_b, xs, g1, tm=n_rows)
        fhist = jnp.transpose(state_ffn[l], (1, 0, 2)).reshape(1, (FFN_CONV_W - 1) * db, D_FF)
        xs, ftail = _ffn(xs, sh2, sc2, g2, n2, wg_b, wu_b, wd_b, fcw, fcb, fhist, step=db, tm=n_rows)
        outs["ks"].append(to_seq(kf).reshape(db, dt, N_HEADS, HEAD_DIM))
        outs["vs"].append(to_seq(vf).reshape(db, dt, N_HEADS, HEAD_DIM))
        outs["cs"].append(jnp.transpose(ctail.reshape(CONV_W - 1, db, W_B), (1, 0, 2)))
        outs["fs"].append(jnp.transpose(ftail.reshape(FFN_CONV_W - 1, db, D_FF), (1, 0, 2)))
        outs["gs"].append(to_seq(vsg))

    ys = to_seq(xs)
    st = {n: jnp.stack(v) for n, v in outs.items()}
    return (xp, ys, st["kp"], st["vp"], st["ks"], st["vs"], st["cp"], st["cs"], st["fp"], st["fs"], st["gs"])
```

```python
import functools

import jax
import jax.numpy as jnp
from jax import lax
from jax.experimental import pallas as pl
from jax.experimental.pallas import tpu as pltpu

F32 = jnp.float32
BF16 = jnp.bfloat16

D_MODEL = 1024
DEPTH = 2
HEAD_DIM = 64
W_A = 512
N_HEADS = 8
W_B = 256
W_C = 256
N_GROUPS_C = 4
CONV_W = 31
CHUNK = 128
QBLOCK = 128
FFN_CONV_W = 3
D_FF = 2816
IN_WIDTH = 3 * W_A + 2 * W_B + 2 * W_C
EPS = 1e-6
PAGE = 128

MASKED_LOG_WEIGHT = -1e30
PROMPT_GK = 8
SAMPLE_PAGES_PER_STEP = 16

LANES = 128
VMEM_LIMIT = 56 * 1024 * 1024


def _params(n_axes, vmem=None):
    return pltpu.CompilerParams(dimension_semantics=("arbitrary",) * n_axes,
                                vmem_limit_bytes=vmem)


def _full(shape):
    return pl.BlockSpec(shape, lambda *_: (0,) * len(shape))


def _sigmoid(x):
    return 1.0 / (1.0 + jnp.exp(-x))


def _gelu_tanh(x):
    c = 0.7978845608028654
    return x * (0.5 * (1.0 + jnp.tanh(c * (x + 0.044715 * (x * x * x)))))


def _layer_norm(x, g, b):
    xc = x - jnp.mean(x, axis=-1, keepdims=True)
    return xc * lax.rsqrt(jnp.mean(xc * xc, axis=-1, keepdims=True) + EPS) * g + b


def _softplus(z):
    neg_abs = lax.bitcast_convert_type(
        lax.bitcast_convert_type(z, jnp.uint32) | jnp.uint32(0x80000000), F32)
    return jnp.maximum(z, 0.0) + jnp.log(1.0 + jnp.exp(neg_abs))


def _neg_suffix_matrix():
    r = jnp.arange(2 * LANES)
    later = (r[:, None] % LANES) > r[None, :]
    return -jnp.where(r[None, :] < LANES, later, True).astype(BF16)


def _stick_block(z, neg_ss, mask=None, split=True):
    sp = _softplus(z)
    if mask is not None:
        sp = jnp.where(mask, sp, 0.0)
    hi = sp.astype(BF16)
    if split:
        lo = (sp - hi.astype(F32)).astype(BF16)
        res = jnp.dot(jnp.concatenate([hi, lo], axis=1), neg_ss, preferred_element_type=F32)
    else:
        res = jnp.dot(hi, neg_ss[:LANES], preferred_element_type=F32)
    return z - sp + res[:, :LANES], res[:, LANES:]


def _stick_weights(z, carry, neg_ss, mask=None):
    arg, tot = _stick_block(z, neg_ss, mask)
    w = jnp.exp(arg + carry)
    if mask is not None:
        w = jnp.where(mask, w, 0.0)
    return w.astype(BF16), carry + tot


def _mod_kernel(c_ref, w_ref, b_ref, o_ref):
    c = c_ref[...]
    s = (c * _sigmoid(c)).astype(BF16)
    o_ref[0] = jnp.dot(s, w_ref[0].astype(BF16), preferred_element_type=F32) + b_ref[0]


def _modulation(c_all, ada_w, ada_b):
    rows = c_all.shape[0]
    tn = 1536
    return pl.pallas_call(
        _mod_kernel,
        grid=(DEPTH, 6 * D_MODEL // tn),
        in_specs=[_full((rows, D_MODEL)),
                  pl.BlockSpec((1, D_MODEL, tn), lambda l, n: (l, 0, n)),
                  pl.BlockSpec((1, 1, tn), lambda l, n: (l, 0, n))],
        out_specs=pl.BlockSpec((1, rows, tn), lambda l, n: (l, 0, n)),
        out_shape=jax.ShapeDtypeStruct((DEPTH, rows, 6 * D_MODEL), F32),
        compiler_params=_params(2, VMEM_LIMIT),
        name="modulation",
    )(c_all, ada_w, ada_b.reshape(DEPTH, 1, 6 * D_MODEL))


def _pre_kernel(x_ref, sh_ref, sc_ref, g_ref, win_ref, qg_ref, kg_ref, gmat_ref, lng_ref, lnb_ref,
                q_ref, kf_ref, vf_ref, kb_ref, vb_ref, glu_ref, gu_ref, vsg_ref):
    x = x_ref[0]
    y = x * lax.rsqrt(jnp.mean(x * x, axis=-1, keepdims=True) + EPS) * g_ref[...]
    h = (y * (1.0 + sc_ref[0]) + sh_ref[0]).astype(BF16)

    def proj(lo, hi):
        return jnp.dot(h, win_ref[:, lo:hi], preferred_element_type=F32)

    def head_norm(t, g):
        ms = jnp.dot((t * t).astype(BF16), gmat_ref[...], preferred_element_type=F32)
        return t * lax.rsqrt(ms + EPS) * g

    q = head_norm(proj(0, W_A), qg_ref[...])
    q_ref[0] = (q * (HEAD_DIM ** -0.5)).astype(BF16)
    k = head_norm(proj(W_A, 2 * W_A), kg_ref[...])
    kf_ref[0] = k
    kb_ref[0] = k.astype(BF16)
    v = proj(2 * W_A, 3 * W_A)
    vf_ref[0] = v
    vb_ref[0] = v.astype(BF16)
    o = 3 * W_A
    glu_ref[0] = proj(o, o + W_B) * _sigmoid(proj(o + W_B, o + 2 * W_B))
    o += 2 * W_B
    gu_ref[0] = _gelu_tanh(proj(o, o + W_C))
    vsg_ref[0] = _layer_norm(_gelu_tanh(proj(o + W_C, o + 2 * W_C)), lng_ref[...], lnb_ref[...])


def _pre(x, sh, sc, norm_g, w_in, qg, kg, gmat, lng, lnb, tm):
    bx, tx, _ = x.shape
    rm = sh.shape[1]
    mod_spec = (pl.BlockSpec((1, 1, D_MODEL), lambda b, i: (b, 0, 0)) if rm == 1 else
                pl.BlockSpec((1, tm, D_MODEL), lambda b, i: (b, i, 0)))

    def row(w):
        return pl.BlockSpec((1, tm, w), lambda b, i: (b, i, 0))

    widths = (W_A, W_A, W_A, W_A, W_A, W_B, W_C, W_C)
    dtypes = (BF16, F32, F32, BF16, BF16, F32, F32, F32)
    return pl.pallas_call(
        _pre_kernel,
        grid=(bx, tx // tm),
        in_specs=[row(D_MODEL), mod_spec, mod_spec, _full((1, D_MODEL)), _full((D_MODEL, IN_WIDTH)),
                  _full((1, W_A)), _full((1, W_A)), _full((W_A, W_A)), _full((1, W_C)), _full((1, W_C))],
        out_specs=[row(w) for w in widths],
        out_shape=[jax.ShapeDtypeStruct((bx, tx, w), d) for w, d in zip(widths, dtypes)],
        compiler_params=_params(2, VMEM_LIMIT),
        name="pre",
    )(x, sh, sc, norm_g, w_in, qg, kg, gmat, lng, lnb)


def _attn_prompt_kernel(bias_ref, q_ref, kt_ref, v_ref, ss_ref, o_ref, w_ref, tot_ref, carry_ref, acc_ref,
                        *, gk):
    p = pl.program_id(1)
    i = pl.program_id(2)
    gkeys = gk * QBLOCK
    q = q_ref[0].astype(F32)
    lane = lax.broadcasted_iota(jnp.int32, (QBLOCK, LANES), 1)
    first = lane < HEAD_DIM
    neg_ss = ss_ref[...]
    g_diag = i // gk

    def query(h):
        qh = jnp.where(first, q, 0.0) if h == 0 else jnp.where(first, 0.0, q)
        b = [bias_ref[(2 * p + h) * 3 + j] for j in range(3)]
        off = jnp.where(lane == 0, b[0], jnp.where(lane == 1, b[1], jnp.where(lane == 2, b[2], 0.0)))
        return jnp.concatenate([qh, off], axis=1).astype(BF16)

    qs = (query(0), query(1))

    def scores(g):
        kt = kt_ref[0, 0, g]
        return tuple(jnp.dot(qs[h], kt, preferred_element_type=F32) for h in range(2))

    def logits(zs, mask):
        run = [None, None]
        for blk in reversed(range(gk)):
            cols = slice(blk * QBLOCK, (blk + 1) * QBLOCK)
            m = None if mask is None else mask[:, cols]
            for h in range(2):
                a, tot = _stick_block(zs[h][:, cols], neg_ss, m, split=False)
                if m is not None:
                    a = jnp.where(m, a, MASKED_LOG_WEIGHT)
                w_ref[h, :, cols] = jnp.exp(a if run[h] is None else a + run[h]).astype(BF16)
                run[h] = tot if run[h] is None else run[h] + tot
        for h in range(2):
            tot_ref[h] = run[h]

    def weigh(g):
        v = v_ref[0, pl.ds(pl.multiple_of(g * gkeys, gkeys), gkeys), :]
        for h in range(2):
            carry = carry_ref[h]
            acc_ref[h] += jnp.exp(carry) * jnp.dot(w_ref[h], v, preferred_element_type=F32)
            carry_ref[h] = carry + tot_ref[h]

    kpos = g_diag * gkeys + lax.broadcasted_iota(jnp.int32, (QBLOCK, gkeys), 1)
    qpos = i * QBLOCK + lax.broadcasted_iota(jnp.int32, (QBLOCK, gkeys), 0)
    carry_ref[...] = jnp.zeros_like(carry_ref)
    acc_ref[...] = jnp.zeros_like(acc_ref)
    logits(scores(g_diag), kpos < qpos)

    @pl.loop(0, g_diag)
    def _(it):
        g = g_diag - 1 - it
        zs = scores(g)
        weigh(g + 1)
        logits(zs, None)

    weigh(0)
    o_ref[0] = jnp.where(first, acc_ref[0], acc_ref[1])


def _grouped_kt(kb, gk):
    b, t, _ = kb.shape
    gkeys = gk * QBLOCK
    kt = jnp.transpose(kb.reshape(b, t // gkeys, gkeys, W_A // LANES, LANES), (0, 3, 1, 4, 2))
    ones_rows = (jnp.arange(LANES) < 3).astype(BF16)[:, None]
    return jnp.concatenate([kt, jnp.broadcast_to(ones_rows, kt.shape[:3] + (LANES, gkeys))], axis=3)


def _bf16_pieces(x):
    hi = x.astype(BF16).astype(F32)
    mid = (x - hi).astype(BF16).astype(F32)
    lo = (x - hi - mid).astype(BF16).astype(F32)
    return jnp.stack([hi, mid, lo], axis=-1).reshape(-1)


def _attn_prompt(q, kt, v, bias, neg_ss, gk):
    b, t, _ = q.shape
    n_groups = t // (gk * QBLOCK)
    grid_spec = pltpu.PrefetchScalarGridSpec(
        num_scalar_prefetch=1,
        grid=(b, W_A // LANES, t // QBLOCK),
        in_specs=[pl.BlockSpec((1, QBLOCK, LANES), lambda b, p, i, s: (b, i, p)),
                  pl.BlockSpec((1, 1, n_groups, 2 * LANES, gk * QBLOCK), lambda b, p, i, s: (b, p, 0, 0, 0)),
                  pl.BlockSpec((1, t, LANES), lambda b, p, i, s: (b, 0, p)),
                  pl.BlockSpec((2 * LANES, 2 * LANES), lambda b, p, i, s: (0, 0))],
        out_specs=pl.BlockSpec((1, QBLOCK, LANES), lambda b, p, i, s: (b, i, p)),
        scratch_shapes=[pltpu.VMEM((2, QBLOCK, gk * QBLOCK), BF16)] + [pltpu.VMEM((2, QBLOCK, LANES), F32)] * 3,
    )
    return pl.pallas_call(
        functools.partial(_attn_prompt_kernel, gk=gk),
        grid_spec=grid_spec,
        out_shape=jax.ShapeDtypeStruct((b, t, W_A), F32),
        compiler_params=_params(3, VMEM_LIMIT),
        name="attn_prompt",
    )(bias, q, kt, v, neg_ss)


def _attn_sample_kernel(pt_ref, qbd_ref, bias_ref, kn_ref, vn_ref, ss_ref, *refs, pp, n_steps):
    del pt_ref
    k_refs, v_refs = refs[:pp], refs[pp:2 * pp]
    o_ref, acc_ref, carry_ref = refs[2 * pp:]
    s = pl.program_id(1)
    qbd = qbd_ref[0]
    bias = bias_ref[...]
    neg_ss = ss_ref[...]
    nrow = qbd.shape[0]

    @pl.when(s == 0)
    def _():
        lane = lax.broadcasted_iota(jnp.int32, (nrow, LANES), 1)
        row = lax.broadcasted_iota(jnp.int32, (nrow, LANES), 0)
        pad = jnp.zeros((PAGE - 8, W_A), F32)
        kn = jnp.concatenate([kn_ref[0], pad], axis=0).astype(BF16)
        vn = jnp.concatenate([vn_ref[0], pad], axis=0).astype(BF16)
        z = lax.dot_general(qbd, kn, (((1,), (1,)), ((), ())), preferred_element_type=F32) + bias
        w, carry0 = _stick_weights(z, jnp.zeros((nrow, LANES), F32), neg_ss, lane < row // N_HEADS)
        acc_ref[...] = jnp.dot(w, vn, preferred_element_type=F32)
        carry_ref[...] = carry0

    kcat = jnp.concatenate([r[0, 0].astype(BF16) for r in k_refs], axis=1)
    vcat = jnp.concatenate([r[0, 0].astype(BF16) for r in v_refs], axis=1)
    z_all = jnp.dot(qbd, kcat, preferred_element_type=F32)
    carry = carry_ref[...]
    ws = []
    for p in range(pp):
        w, carry = _stick_weights(z_all[:, p * LANES:(p + 1) * LANES] + bias, carry, neg_ss)
        ws.append(w)
    carry_ref[...] = carry
    w_all = jnp.concatenate(ws, axis=1)
    acc = acc_ref[...] + lax.dot_general(w_all, vcat, (((1,), (1,)), ((), ())), preferred_element_type=F32)
    acc_ref[...] = acc

    @pl.when(s == n_steps - 1)
    def _():
        rows = lax.broadcasted_iota(jnp.int32, acc.shape, 0)
        cols = lax.broadcasted_iota(jnp.int32, acc.shape, 1)
        own = jnp.where(rows % N_HEADS == cols // HEAD_DIM, acc, 0.0)
        o_ref[0] = jnp.sum(own.reshape(nrow // N_HEADS, N_HEADS, W_A), axis=1)


def _attn_sample(qbd, bias_rows, k_new, v_new, neg_ss, kt_pool, vt_pool, page_table, layer, pp):
    nb, nrow, _ = qbd.shape
    n_pages = page_table.shape[1]
    n_steps = n_pages // pp

    def page_spec(p):
        return pl.BlockSpec(
            (1, 1, W_A, PAGE),
            lambda b, s, pt: (layer, pt[b, n_pages - 1 - s * pp - p], 0, 0))

    grid_spec = pltpu.PrefetchScalarGridSpec(
        num_scalar_prefetch=1,
        grid=(nb, n_steps),
        in_specs=[pl.BlockSpec((1, nrow, W_A), lambda b, s, pt: (b, 0, 0)),
                  pl.BlockSpec((nrow, LANES), lambda b, s, pt: (0, 0)),
                  pl.BlockSpec((1, 8, W_A), lambda b, s, pt: (b, 0, 0)),
                  pl.BlockSpec((1, 8, W_A), lambda b, s, pt: (b, 0, 0)),
                  pl.BlockSpec((2 * LANES, 2 * LANES), lambda b, s, pt: (0, 0))]
                 + [page_spec(p) for p in range(pp)] * 2,
        out_specs=pl.BlockSpec((1, nrow // N_HEADS, W_A), lambda b, s, pt: (b, 0, 0)),
        scratch_shapes=[pltpu.VMEM((nrow, W_A), F32), pltpu.VMEM((nrow, LANES), F32)],
    )
    return pl.pallas_call(
        functools.partial(_attn_sample_kernel, pp=pp, n_steps=n_steps),
        grid_spec=grid_spec,
        out_shape=jax.ShapeDtypeStruct((nb, nrow // N_HEADS, W_A), F32),
        compiler_params=_params(2, VMEM_LIMIT),
        name="attn_sample",
    )(page_table, qbd, bias_rows, k_new, v_new, neg_ss, *([kt_pool] * pp), *([vt_pool] * pp))


def _conv_module(glu_ref, hist_ref, w_ref, b_ref, lng_ref, lnb_ref, tail_ref, xc_ref, *, step, hp, tm, carry_tiles):
    @pl.when(pl.program_id(1) == 0)
    def _():
        xc_ref[0:hp] = hist_ref[0]

    xc_ref[hp:hp + tm] = glu_ref[0]
    acc = jnp.broadcast_to(b_ref[...], (tm, W_B))
    base = hp - (CONV_W - 1) * step
    for k in range(CONV_W):
        acc = acc + w_ref[k:k + 1, :] * xc_ref[base + k * step:base + k * step + tm, :]
    y = _layer_norm(acc, lng_ref[...], lnb_ref[...])
    tail = xc_ref[tm:tm + hp]
    tail_ref[0] = tail
    if carry_tiles:
        xc_ref[0:hp] = tail
    return y * _sigmoid(y)


def _mix_kernel(oa_ref, glu_ref, hist_ref, cw_ref, cb_ref, clg_ref, clb_ref, gu_ref, vsg_ref, wm_ref, bm_ref,
                wo_ref, x_ref, g1_ref, o_ref, tail_ref, xc_ref, *, tm, step, hp, carry_tiles):
    ob = _conv_module(glu_ref, hist_ref, cw_ref, cb_ref, clg_ref, clb_ref, tail_ref, xc_ref,
                      step=step, hp=hp, tm=tm, carry_tiles=carry_tiles)
    lane = lax.broadcasted_iota(jnp.int32, (CHUNK, W_C), 1)
    parts = []
    for c in range(tm // CHUNK):
        rows = slice(c * CHUNK, (c + 1) * CHUNK)
        vs = vsg_ref[0, rows, :].astype(BF16)
        sp = bm_ref[...]
        for g in range(N_GROUPS_C):
            r = jnp.dot(wm_ref[g], vs, preferred_element_type=F32)
            sp = sp + jnp.where(lane // HEAD_DIM == g, r, 0.0)
        parts.append(gu_ref[0, rows, :] * sp)
    oc = jnp.concatenate(parts, axis=0) if len(parts) > 1 else parts[0]
    cat = jnp.concatenate([oa_ref[0], ob, oc], axis=1).astype(BF16)
    y = jnp.dot(cat, wo_ref[...], preferred_element_type=F32)
    o_ref[0] = x_ref[0] + g1_ref[0] * y


def _mix(oa, glu, hist, cw, cb, clg, clb, gu, vsg, wm, bm, wo, x, g1, step, tm):
    bx, tx, _ = x.shape
    hp = hist.shape[1]
    rm = g1.shape[1]
    mod_spec = (pl.BlockSpec((1, 1, D_MODEL), lambda b, i: (b, 0, 0)) if rm == 1 else
                pl.BlockSpec((1, tm, D_MODEL), lambda b, i: (b, i, 0)))
    hist_spec = pl.BlockSpec((1, hp, W_B), lambda b, i: (b, 0, 0))

    def row(w):
        return pl.BlockSpec((1, tm, w), lambda b, i: (b, i, 0))

    return pl.pallas_call(
        functools.partial(_mix_kernel, tm=tm, step=step, hp=hp, carry_tiles=tx > tm),
        grid=(bx, tx // tm),
        in_specs=[row(W_A), row(W_B), hist_spec, _full((32, W_B)), _full((1, W_B)), _full((1, W_B)),
                  _full((1, W_B)), row(W_C), row(W_C), _full((N_GROUPS_C, CHUNK, CHUNK)),
                  _full((CHUNK, W_C)), _full((D_MODEL, D_MODEL)), row(D_MODEL), mod_spec],
        out_specs=[row(D_MODEL), hist_spec],
        out_shape=[jax.ShapeDtypeStruct((bx, tx, D_MODEL), F32), jax.ShapeDtypeStruct((bx, hp, W_B), F32)],
        scratch_shapes=[pltpu.VMEM((hp + tm, W_B), F32)],
        compiler_params=_params(2, VMEM_LIMIT),
        name="mix",
    )(oa, glu, hist, cw, cb, clg, clb, gu, vsg, wm, bm, wo, x, g1)


def _ffn_kernel(x_ref, sh_ref, sc_ref, g2_ref, ng_ref, wg_ref, wu_ref, wd_ref, cw_ref, cb_ref, hist_ref,
                o_ref, tail_ref, xc_ref, *, step, hp, tm, carry_tiles):
    @pl.when(pl.program_id(1) == 0)
    def _():
        xc_ref[0:hp] = hist_ref[0]

    x = x_ref[0]
    y = x * lax.rsqrt(jnp.mean(x * x, axis=-1, keepdims=True) + EPS) * ng_ref[...]
    h = (y * (1.0 + sc_ref[0]) + sh_ref[0]).astype(BF16)
    gate = jnp.dot(h, wg_ref[...], preferred_element_type=F32)
    xc_ref[hp:hp + tm] = gate
    up = jnp.dot(h, wu_ref[...], preferred_element_type=F32)
    gc = (cw_ref[0:1, :] * xc_ref[hp - 2 * step:hp - 2 * step + tm, :]
          + cw_ref[1:2, :] * xc_ref[hp - step:hp - step + tm, :]
          + cw_ref[2:3, :] * gate + cb_ref[...])
    act = (gc * _sigmoid(gc) * up).astype(BF16)
    o_ref[0] = x + g2_ref[0] * jnp.dot(act, wd_ref[...], preferred_element_type=F32)
    tail = xc_ref[tm:tm + hp]
    tail_ref[0] = tail
    if carry_tiles:
        xc_ref[0:hp] = tail


def _ffn(x, sh, sc, g2, ng, wg, wu, wd, cw, cb, hist, step, tm):
    bx, tx, _ = x.shape
    hp = hist.shape[1]
    rm = sh.shape[1]
    mod_spec = (pl.BlockSpec((1, 1, D_MODEL), lambda b, i: (b, 0, 0)) if rm == 1 else
                pl.BlockSpec((1, tm, D_MODEL), lambda b, i: (b, i, 0)))
    row = pl.BlockSpec((1, tm, D_MODEL), lambda b, i: (b, i, 0))
    return pl.pallas_call(
        functools.partial(_ffn_kernel, step=step, hp=hp, tm=tm, carry_tiles=tx > tm),
        grid=(bx, tx // tm),
        in_specs=[row, mod_spec, mod_spec, mod_spec, _full((1, D_MODEL)),
                  _full((D_MODEL, D_FF)), _full((D_MODEL, D_FF)), _full((D_FF, D_MODEL)),
                  _full((8, D_FF)), _full((1, D_FF)),
                  pl.BlockSpec((1, hp, D_FF), lambda b, i: (b, 0, 0))],
        out_specs=[row, pl.BlockSpec((1, hp, D_FF), lambda b, i: (b, 0, 0))],
        out_shape=[jax.ShapeDtypeStruct((bx, tx, D_MODEL), F32), jax.ShapeDtypeStruct((bx, hp, D_FF), F32)],
        scratch_shapes=[pltpu.VMEM((hp + tm, D_FF), F32)],
        compiler_params=_params(2, VMEM_LIMIT),
        name="ffn",
    )(x, sh, sc, g2, ng, wg, wu, wd, cw, cb, hist)


def kernel(x_prompt, x_sample, cache_k, cache_v, state_conv, state_ffn, page_table, c_prompt, c_sample,
           ada_w, ada_b, norm1_g, norm2_g, w_in, q_norm_g, k_norm_g, sb_bias, conv_w, conv_b, conv_ln_g,
           conv_ln_b, sg_ln_g, sg_ln_b, sg_w, sg_b, w_o, ffn_wg, ffn_wu, ffn_conv_w, ffn_conv_b, ffn_wd):
    nb, seq, _ = x_prompt.shape
    db, dt, _ = x_sample.shape
    n_rows = db * dt
    n_pool = cache_k.shape[1]

    c_all = jnp.concatenate([c_prompt, c_sample], axis=0)
    c_pad = (-c_all.shape[0]) % 8
    c_all = jnp.pad(c_all, ((0, c_pad), (0, 0)))
    mod = _modulation(c_all, ada_w, ada_b)

    neg_ss = _neg_suffix_matrix()
    h_of = jnp.arange(W_A) // HEAD_DIM
    gmat = jnp.where(h_of[:, None] == h_of[None, :], 1.0 / HEAD_DIM, 0.0).astype(BF16)
    eye_h = jnp.eye(N_HEADS, dtype=bool)
    tri = jnp.tril(jnp.ones((CHUNK, CHUNK), bool))

    kt_pool = jnp.transpose(cache_k, (0, 1, 3, 4, 2)).reshape(DEPTH, n_pool, W_A, PAGE)
    vt_pool = jnp.transpose(cache_v, (0, 1, 3, 4, 2)).reshape(DEPTH, n_pool, W_A, PAGE)

    xp = x_prompt
    xs = jnp.transpose(x_sample, (1, 0, 2)).reshape(1, n_rows, D_MODEL)
    outs = {n: [] for n in ("kp", "vp", "ks", "vs", "cp", "cs", "fp", "fs", "gs")}
    zeros_conv = jnp.zeros((nb, 32, W_B), F32)
    zeros_ffn = jnp.zeros((nb, 8, D_FF), F32)

    def to_seq(a):
        return jnp.transpose(a.reshape(dt, db, a.shape[-1]), (1, 0, 2))

    def to_tm(a):
        return jnp.transpose(a, (1, 0, 2)).reshape(1, n_rows, a.shape[-1])

    for l in range(DEPTH):
        mod_p = [m[:, None, :] for m in jnp.split(mod[l, :nb], 6, axis=-1)]
        mod_s = [jnp.tile(m, (dt, 1))[None] for m in jnp.split(mod[l, nb:nb + db], 6, axis=-1)]
        w_in_b = w_in[l].astype(BF16)
        wo_b = w_o[l].astype(BF16)
        wg_b, wu_b, wd_b = ffn_wg[l].astype(BF16), ffn_wu[l].astype(BF16), ffn_wd[l].astype(BF16)
        qg = jnp.tile(q_norm_g[l], N_HEADS)[None]
        kg = jnp.tile(k_norm_g[l], N_HEADS)[None]
        n1, n2 = norm1_g[l][None], norm2_g[l][None]
        lng, lnb = sg_ln_g[l][None], sg_ln_b[l][None]
        cw = jnp.pad(conv_w[l], ((0, 1), (0, 0)))
        cb, clg, clb = conv_b[l][None], conv_ln_g[l][None], conv_ln_b[l][None]
        fcw = jnp.pad(ffn_conv_w[l], ((0, 5), (0, 0)))
        fcb = ffn_conv_b[l][None]

        sh1, sc1, g1, sh2, sc2, g2 = mod_p
        q, kf, vf, kb, vb, glu, gu, vsg = _pre(xp, sh1, sc1, n1, w_in_b, qg, kg, gmat, lng, lnb, tm=512)
        oa = _attn_prompt(q, _grouped_kt(kb, PROMPT_GK), vb, _bf16_pieces(sb_bias[l]), neg_ss, PROMPT_GK)
        wm = jnp.where(tri[None], sg_w[l], 0.0).astype(BF16)
        bm = jnp.repeat(sg_b[l].T, HEAD_DIM, axis=1)
        xp, ctail = _mix(oa, glu, zeros_conv, cw, cb, clg, clb, gu, vsg, wm, bm, wo_b, xp, g1, step=1, tm=512)
        xp, ftail = _ffn(xp, sh2, sc2, g2, n2, wg_b, wu_b, wd_b, fcw, fcb, zeros_ffn, step=1, tm=256)
        outs["kp"].append(kf.reshape(nb, seq, N_HEADS, HEAD_DIM))
        outs["vp"].append(vf.reshape(nb, seq, N_HEADS, HEAD_DIM))
        outs["cp"].append(ctail[:, 32 - (CONV_W - 1):])
        outs["fp"].append(ftail[:, 8 - (FFN_CONV_W - 1):])

        sh1, sc1, g1, sh2, sc2, g2 = mod_s
        q, kf, vf, kb, vb, glu, gu, vsg = _pre(xs, sh1, sc1, n1, w_in_b, qg, kg, gmat, lng, lnb, tm=n_rows)
        q_seq = to_seq(q).reshape(db, dt, N_HEADS, HEAD_DIM)
        qbd = jnp.where(eye_h[None, None, :, :, None], q_seq[:, :, None, :, :],
                        jnp.zeros((), BF16)).reshape(db, dt * N_HEADS, W_A)
        bias_rows = jnp.broadcast_to(jnp.tile(sb_bias[l], dt)[:, None], (dt * N_HEADS, LANES))
        k_new = jnp.pad(to_seq(kb).astype(F32), ((0, 0), (0, 8 - dt), (0, 0)))
        v_new = jnp.pad(to_seq(vb).astype(F32), ((0, 0), (0, 8 - dt), (0, 0)))
        oa = to_tm(_attn_sample(qbd, bias_rows, k_new, v_new, neg_ss, kt_pool, vt_pool, page_table, l,
                                SAMPLE_PAGES_PER_STEP))
        hist = jnp.transpose(state_conv[l], (1, 0, 2)).reshape(1, (CONV_W - 1) * db, W_B)
        w4 = jnp.where(tri[:dt, :dt][None], sg_w[l][:, :dt, :dt], 0.0)
        wm = jnp.einsum("gts,bc->gtbsc", w4, jnp.eye(db, dtype=F32)).reshape(N_GROUPS_C, n_rows, n_rows)
        bm = jnp.repeat(jnp.repeat(sg_b[l][:, :dt].T, db, axis=0), HEAD_DIM, axis=1)
        xs, ctail = _mix(oa, glu, hist, cw, cb, clg, clb, gu, vsg, wm.astype(BF16), bm, wo_b, xs, g1,
                         step=db, tm=n_rows)
        fhist = jnp.transpose(state_ffn[l], (1, 0, 2)).reshape(1, (FFN_CONV_W - 1) * db, D_FF)
        xs, ftail = _ffn(xs, sh2, sc2, g2, n2, wg_b, wu_b, wd_b, fcw, fcb, fhist, step=db, tm=n_rows)
        outs["ks"].append(to_seq(kf).reshape(db, dt, N_HEADS, HEAD_DIM))
        outs["vs"].append(to_seq(vf).reshape(db, dt, N_HEADS, HEAD_DIM))
        outs["cs"].append(jnp.transpose(ctail.reshape(CONV_W - 1, db, W_B), (1, 0, 2)))
        outs["fs"].append(jnp.transpose(ftail.reshape(FFN_CONV_W - 1, db, D_FF), (1, 0, 2)))
        outs["gs"].append(to_seq(vsg))

    ys = to_seq(xs)
    st = {n: jnp.stack(v) for n, v in outs.items()}
    return (xp, ys, st["kp"], st["vp"], st["ks"], st["vs"], st["cp"], st["cs"], st["fp"], st["fs"], st["gs"])
```

```python
import functools

import jax
import jax.numpy as jnp
from jax import lax
from jax.experimental import pallas as pl
from jax.experimental.pallas import tpu as pltpu

F32 = jnp.float32
BF16 = jnp.bfloat16

D_MODEL = 1024
DEPTH = 2
HEAD_DIM = 64
W_A = 512
N_HEADS = 8
W_B = 256
W_C = 256
N_GROUPS_C = 4
CONV_W = 31
CHUNK = 128
FFN_CONV_W = 3
D_FF = 2816
IN_WIDTH = 3 * W_A + 2 * W_B + 2 * W_C
EPS = 1e-6
PAGE = 128

MASKED_LOG_WEIGHT = -1e30
PROMPT_GK = 4
PROMPT_TQ = 256
SAMPLE_PAGES_PER_STEP = 16

LANES = 128
VMEM_LIMIT = 56 * 1024 * 1024


def _params(n_axes, vmem=None):
    return pltpu.CompilerParams(dimension_semantics=("arbitrary",) * n_axes,
                                vmem_limit_bytes=vmem)


def _full(shape):
    return pl.BlockSpec(shape, lambda *_: (0,) * len(shape))


def _sigmoid(x):
    return 1.0 / (1.0 + jnp.exp(-x))


def _gelu_tanh(x):
    c = 0.7978845608028654
    return x * (0.5 * (1.0 + jnp.tanh(c * (x + 0.044715 * (x * x * x)))))


def _layer_norm(x, g, b):
    xc = x - jnp.mean(x, axis=-1, keepdims=True)
    return xc * lax.rsqrt(jnp.mean(xc * xc, axis=-1, keepdims=True) + EPS) * g + b


def _softplus(z):
    neg_abs = lax.bitcast_convert_type(
        lax.bitcast_convert_type(z, jnp.uint32) | jnp.uint32(0x80000000), F32)
    return jnp.maximum(z, 0.0) + jnp.log(1.0 + jnp.exp(neg_abs))


def _neg_suffix_matrix():
    r = jnp.arange(2 * LANES)
    later = (r[:, None] % LANES) > r[None, :]
    return -jnp.where(r[None, :] < LANES, later, True).astype(BF16)


def _stick_block(z, neg_ss, mask=None, split=True):
    sp = _softplus(z)
    if mask is not None:
        sp = jnp.where(mask, sp, 0.0)
    hi = sp.astype(BF16)
    if split:
        lo = (sp - hi.astype(F32)).astype(BF16)
        res = jnp.dot(jnp.concatenate([hi, lo], axis=1), neg_ss, preferred_element_type=F32)
    else:
        res = jnp.dot(hi, neg_ss[:LANES], preferred_element_type=F32)
    return z - sp + res[:, :LANES], res[:, LANES:]


def _stick_weights(z, carry, neg_ss, mask=None):
    arg, tot = _stick_block(z, neg_ss, mask)
    w = jnp.exp(arg + carry)
    if mask is not None:
        w = jnp.where(mask, w, 0.0)
    return w.astype(BF16), carry + tot


def _mod_kernel(c_ref, w_ref, b_ref, o_ref):
    c = c_ref[...]
    s = (c * _sigmoid(c)).astype(BF16)
    o_ref[0] = jnp.dot(s, w_ref[0].astype(BF16), preferred_element_type=F32) + b_ref[0]


def _modulation(c_all, ada_w, ada_b):
    rows = c_all.shape[0]
    tn = 1536
    return pl.pallas_call(
        _mod_kernel,
        grid=(DEPTH, 6 * D_MODEL // tn),
        in_specs=[_full((rows, D_MODEL)),
                  pl.BlockSpec((1, D_MODEL, tn), lambda l, n: (l, 0, n)),
                  pl.BlockSpec((1, 1, tn), lambda l, n: (l, 0, n))],
        out_specs=pl.BlockSpec((1, rows, tn), lambda l, n: (l, 0, n)),
        out_shape=jax.ShapeDtypeStruct((DEPTH, rows, 6 * D_MODEL), F32),
        compiler_params=_params(2, VMEM_LIMIT),
        name="modulation",
    )(c_all, ada_w, ada_b.reshape(DEPTH, 1, 6 * D_MODEL))


def _pre_kernel(x_ref, sh_ref, sc_ref, g_ref, win_ref, qg_ref, kg_ref, gmat_ref, lng_ref, lnb_ref,
                q_ref, kf_ref, vf_ref, kb_ref, vb_ref, glu_ref, gu_ref, vsg_ref):
    x = x_ref[0]
    y = x * lax.rsqrt(jnp.mean(x * x, axis=-1, keepdims=True) + EPS) * g_ref[...]
    h = (y * (1.0 + sc_ref[0]) + sh_ref[0]).astype(BF16)

    def proj(lo, hi):
        return jnp.dot(h, win_ref[:, lo:hi], preferred_element_type=F32)

    def head_norm(t, g):
        ms = jnp.dot((t * t).astype(BF16), gmat_ref[...], preferred_element_type=F32)
        return t * lax.rsqrt(ms + EPS) * g

    q = head_norm(proj(0, W_A), qg_ref[...])
    q_ref[0] = (q * (HEAD_DIM ** -0.5)).astype(BF16)
    k = head_norm(proj(W_A, 2 * W_A), kg_ref[...])
    kf_ref[0] = k
    kb_ref[0] = k.astype(BF16)
    v = proj(2 * W_A, 3 * W_A)
    vf_ref[0] = v
    vb_ref[0] = v.astype(BF16)
    o = 3 * W_A
    glu_ref[0] = proj(o, o + W_B) * _sigmoid(proj(o + W_B, o + 2 * W_B))
    o += 2 * W_B
    gu_ref[0] = _gelu_tanh(proj(o, o + W_C))
    vsg_ref[0] = _layer_norm(_gelu_tanh(proj(o + W_C, o + 2 * W_C)), lng_ref[...], lnb_ref[...])


def _pre(x, sh, sc, norm_g, w_in, qg, kg, gmat, lng, lnb, tm):
    bx, tx, _ = x.shape
    rm = sh.shape[1]
    mod_spec = (pl.BlockSpec((1, 1, D_MODEL), lambda b, i: (b, 0, 0)) if rm == 1 else
                pl.BlockSpec((1, tm, D_MODEL), lambda b, i: (b, i, 0)))

    def row(w):
        return pl.BlockSpec((1, tm, w), lambda b, i: (b, i, 0))

    widths = (W_A, W_A, W_A, W_A, W_A, W_B, W_C, W_C)
    dtypes = (BF16, F32, F32, BF16, BF16, F32, F32, F32)
    return pl.pallas_call(
        _pre_kernel,
        grid=(bx, tx // tm),
        in_specs=[row(D_MODEL), mod_spec, mod_spec, _full((1, D_MODEL)), _full((D_MODEL, IN_WIDTH)),
                  _full((1, W_A)), _full((1, W_A)), _full((W_A, W_A)), _full((1, W_C)), _full((1, W_C))],
        out_specs=[row(w) for w in widths],
        out_shape=[jax.ShapeDtypeStruct((bx, tx, w), d) for w, d in zip(widths, dtypes)],
        compiler_params=_params(2, VMEM_LIMIT),
        name="pre",
    )(x, sh, sc, norm_g, w_in, qg, kg, gmat, lng, lnb)


def _attn_prompt_kernel(bias_ref, q_ref, kt_ref, v_ref, ss_ref, o_ref, w_ref, z_ref, tot_ref, carry_ref, acc_ref,
                        *, gk, tq):
    p = pl.program_id(1)
    i = pl.program_id(2)
    gkeys = gk * LANES
    q = q_ref[0].astype(F32)
    lane = lax.broadcasted_iota(jnp.int32, (tq, LANES), 1)
    first = lane < HEAD_DIM
    neg_ss = ss_ref[...]
    g_diag = (i * tq + tq - 1) // gkeys

    def query(h):
        qh = jnp.where(first, q, 0.0) if h == 0 else jnp.where(first, 0.0, q)
        b = [bias_ref[(2 * p + h) * 3 + j] for j in range(3)]
        off = jnp.where(lane == 0, b[0], jnp.where(lane == 1, b[1], jnp.where(lane == 2, b[2], 0.0)))
        return jnp.concatenate([qh, off], axis=1).astype(BF16)

    qs = (query(0), query(1))

    half = gkeys // 2

    def trip(zs, mask, g_scores, g_weigh):
        kt = kt_ref[0, 0, g_scores]
        nxt = [[None, None], [None, None]]
        pv = [[None, None], [None, None]]
        fillers = []
        if g_weigh is not None:
            v = v_ref[0, pl.ds(pl.multiple_of(g_weigh * gkeys, gkeys), gkeys), :]
            wp = (w_ref[0], w_ref[1])

            def weigh_half(h, c):
                keys = slice(c * half, (c + 1) * half)
                pv[h][c] = jnp.dot(wp[h][:, keys], v[keys], preferred_element_type=F32)

            fillers += [functools.partial(weigh_half, h, c) for c in (1, 0) for h in range(2)]

        def score_half(h, c):
            nxt[h][c] = jnp.dot(qs[h], kt[:, c * half:(c + 1) * half], preferred_element_type=F32)

        fillers += [functools.partial(score_half, h, c) for c in (1, 0) for h in range(2)]

        run = [None, None]
        for blk in reversed(range(gk)):
            cols = slice(blk * LANES, (blk + 1) * LANES)
            m = None if mask is None else mask[:, cols]
            for h in range(2):
                if fillers:
                    fillers.pop(0)()
                a, tot = _stick_block(zs[h][:, cols], neg_ss, m, split=False)
                if m is not None:
                    a = jnp.where(m, a, MASKED_LOG_WEIGHT)
                w_ref[h, :, cols] = jnp.exp(a if run[h] is None else a + run[h]).astype(BF16)
                run[h] = tot if run[h] is None else run[h] + tot
        while fillers:
            fillers.pop(0)()
        for h in range(2):
            if g_weigh is not None:
                carry = carry_ref[h]
                acc_ref[h] += jnp.exp(carry) * (pv[h][0] + pv[h][1])
                carry_ref[h] = carry + tot_ref[h]
            tot_ref[h] = run[h]
            z_ref[h] = jnp.concatenate(nxt[h], axis=1)

    kpos = g_diag * gkeys + lax.broadcasted_iota(jnp.int32, (tq, gkeys), 1)
    qpos = i * tq + lax.broadcasted_iota(jnp.int32, (tq, gkeys), 0)
    carry_ref[...] = jnp.zeros_like(carry_ref)
    acc_ref[...] = jnp.zeros_like(acc_ref)
    kt_diag = kt_ref[0, 0, g_diag]
    trip(tuple(jnp.dot(qs[h], kt_diag, preferred_element_type=F32) for h in range(2)), kpos < qpos,
         jnp.maximum(g_diag - 1, 0), None)

    @pl.loop(0, g_diag)
    def _(it):
        g = g_diag - 1 - it
        trip((z_ref[0], z_ref[1]), None, jnp.maximum(g - 1, 0), g + 1)

    v = v_ref[0, 0:gkeys, :]
    for h in range(2):
        acc_ref[h] += jnp.exp(carry_ref[h]) * jnp.dot(w_ref[h], v, preferred_element_type=F32)
    o_ref[0] = jnp.where(first, acc_ref[0], acc_ref[1])


def _grouped_kt(kb, gk):
    b, t, _ = kb.shape
    gkeys = gk * LANES
    kt = jnp.transpose(kb.reshape(b, t // gkeys, gkeys, W_A // LANES, LANES), (0, 3, 1, 4, 2))
    ones_rows = (jnp.arange(LANES) < 3).astype(BF16)[:, None]
    return jnp.concatenate([kt, jnp.broadcast_to(ones_rows, kt.shape[:3] + (LANES, gkeys))], axis=3)


def _bf16_pieces(x):
    hi = x.astype(BF16).astype(F32)
    mid = (x - hi).astype(BF16).astype(F32)
    lo = (x - hi - mid).astype(BF16).astype(F32)
    return jnp.stack([hi, mid, lo], axis=-1).reshape(-1)


def _attn_prompt(q, kt, v, bias, neg_ss, gk, tq):
    b, t, _ = q.shape
    gkeys = gk * LANES
    grid_spec = pltpu.PrefetchScalarGridSpec(
        num_scalar_prefetch=1,
        grid=(b, W_A // LANES, t // tq),
        in_specs=[pl.BlockSpec((1, tq, LANES), lambda b, p, i, s: (b, i, p)),
                  pl.BlockSpec((1, 1, t // gkeys, 2 * LANES, gkeys), lambda b, p, i, s: (b, p, 0, 0, 0)),
                  pl.BlockSpec((1, t, LANES), lambda b, p, i, s: (b, 0, p)),
                  pl.BlockSpec((2 * LANES, 2 * LANES), lambda b, p, i, s: (0, 0))],
        out_specs=pl.BlockSpec((1, tq, LANES), lambda b, p, i, s: (b, i, p)),
        scratch_shapes=[pltpu.VMEM((2, tq, gkeys), BF16), pltpu.VMEM((2, tq, gkeys), F32)]
                       + [pltpu.VMEM((2, tq, LANES), F32)] * 3,
    )
    return pl.pallas_call(
        functools.partial(_attn_prompt_kernel, gk=gk, tq=tq),
        grid_spec=grid_spec,
        out_shape=jax.ShapeDtypeStruct((b, t, W_A), F32),
        compiler_params=_params(3, VMEM_LIMIT),
        name="attn_prompt",
    )(bias, q, kt, v, neg_ss)


def _attn_sample_kernel(pt_ref, qbd_ref, bias_ref, kn_ref, vn_ref, ss_ref, *refs, pp, n_steps):
    del pt_ref
    k_refs, v_refs = refs[:pp], refs[pp:2 * pp]
    o_ref, acc_ref, carry_ref = refs[2 * pp:]
    s = pl.program_id(1)
    qbd = qbd_ref[0]
    bias = bias_ref[...]
    neg_ss = ss_ref[...]
    nrow = qbd.shape[0]

    @pl.when(s == 0)
    def _():
        lane = lax.broadcasted_iota(jnp.int32, (nrow, LANES), 1)
        row = lax.broadcasted_iota(jnp.int32, (nrow, LANES), 0)
        pad = jnp.zeros((PAGE - 8, W_A), F32)
        kn = jnp.concatenate([kn_ref[0], pad], axis=0).astype(BF16)
        vn = jnp.concatenate([vn_ref[0], pad], axis=0).astype(BF16)
        z = lax.dot_general(qbd, kn, (((1,), (1,)), ((), ())), preferred_element_type=F32) + bias
        w, carry0 = _stick_weights(z, jnp.zeros((nrow, LANES), F32), neg_ss, lane < row // N_HEADS)
        acc_ref[...] = jnp.dot(w, vn, preferred_element_type=F32)
        carry_ref[...] = carry0

    kcat = jnp.concatenate([r[0, 0].astype(BF16) for r in k_refs], axis=1)
    vcat = jnp.concatenate([r[0, 0].astype(BF16) for r in v_refs], axis=1)
    z_all = jnp.dot(qbd, kcat, preferred_element_type=F32)
    carry = carry_ref[...]
    ws = []
    for p in range(pp):
        w, carry = _stick_weights(z_all[:, p * LANES:(p + 1) * LANES] + bias, carry, neg_ss)
        ws.append(w)
    carry_ref[...] = carry
    w_all = jnp.concatenate(ws, axis=1)
    acc = acc_ref[...] + lax.dot_general(w_all, vcat, (((1,), (1,)), ((), ())), preferred_element_type=F32)
    acc_ref[...] = acc

    @pl.when(s == n_steps - 1)
    def _():
        rows = lax.broadcasted_iota(jnp.int32, acc.shape, 0)
        cols = lax.broadcasted_iota(jnp.int32, acc.shape, 1)
        own = jnp.where(rows % N_HEADS == cols // HEAD_DIM, acc, 0.0)
        o_ref[0] = jnp.sum(own.reshape(nrow // N_HEADS, N_HEADS, W_A), axis=1)


def _attn_sample(qbd, bias_rows, k_new, v_new, neg_ss, kt_pool, vt_pool, page_table, layer, pp):
    nb, nrow, _ = qbd.shape
    n_pages = page_table.shape[1]
    n_steps = n_pages // pp

    def page_spec(p):
        return pl.BlockSpec(
            (1, 1, W_A, PAGE),
            lambda b, s, pt: (layer, pt[b, n_pages - 1 - s * pp - p], 0, 0))

    grid_spec = pltpu.PrefetchScalarGridSpec(
        num_scalar_prefetch=1,
        grid=(nb, n_steps),
        in_specs=[pl.BlockSpec((1, nrow, W_A), lambda b, s, pt: (b, 0, 0)),
                  pl.BlockSpec((nrow, LANES), lambda b, s, pt: (0, 0)),
                  pl.BlockSpec((1, 8, W_A), lambda b, s, pt: (b, 0, 0)),
                  pl.BlockSpec((1, 8, W_A), lambda b, s, pt: (b, 0, 0)),
                  pl.BlockSpec((2 * LANES, 2 * LANES), lambda b, s, pt: (0, 0))]
                 + [page_spec(p) for p in range(pp)] * 2,
        out_specs=pl.BlockSpec((1, nrow // N_HEADS, W_A), lambda b, s, pt: (b, 0, 0)),
        scratch_shapes=[pltpu.VMEM((nrow, W_A), F32), pltpu.VMEM((nrow, LANES), F32)],
    )
    return pl.pallas_call(
        functools.partial(_attn_sample_kernel, pp=pp, n_steps=n_steps),
        grid_spec=grid_spec,
        out_shape=jax.ShapeDtypeStruct((nb, nrow // N_HEADS, W_A), F32),
        compiler_params=_params(2, VMEM_LIMIT),
        name="attn_sample",
    )(page_table, qbd, bias_rows, k_new, v_new, neg_ss, *([kt_pool] * pp), *([vt_pool] * pp))


def _conv_module(glu_ref, hist_ref, w_ref, b_ref, lng_ref, lnb_ref, tail_ref, xc_ref, *, step, hp, tm, carry_tiles):
    @pl.when(pl.program_id(1) == 0)
    def _():
        xc_ref[0:hp] = hist_ref[0]

    xc_ref[hp:hp + tm] = glu_ref[0]
    acc = jnp.broadcast_to(b_ref[...], (tm, W_B))
    base = hp - (CONV_W - 1) * step
    for k in range(CONV_W):
        acc = acc + w_ref[k:k + 1, :] * xc_ref[base + k * step:base + k * step + tm, :]
    y = _layer_norm(acc, lng_ref[...], lnb_ref[...])
    tail = xc_ref[tm:tm + hp]
    tail_ref[0] = tail
    if carry_tiles:
        xc_ref[0:hp] = tail
    return y * _sigmoid(y)


def _mix_kernel(oa_ref, glu_ref, hist_ref, cw_ref, cb_ref, clg_ref, clb_ref, gu_ref, vsg_ref, wm_ref, bm_ref,
                wo_ref, x_ref, g1_ref, o_ref, tail_ref, xc_ref, *, tm, step, hp, carry_tiles):
    ob = _conv_module(glu_ref, hist_ref, cw_ref, cb_ref, clg_ref, clb_ref, tail_ref, xc_ref,
                      step=step, hp=hp, tm=tm, carry_tiles=carry_tiles)
    lane = lax.broadcasted_iota(jnp.int32, (CHUNK, W_C), 1)
    parts = []
    for c in range(tm // CHUNK):
        rows = slice(c * CHUNK, (c + 1) * CHUNK)
        vs = vsg_ref[0, rows, :].astype(BF16)
        sp = bm_ref[...]
        for g in range(N_GROUPS_C):
            r = jnp.dot(wm_ref[g], vs, preferred_element_type=F32)
            sp = sp + jnp.where(lane // HEAD_DIM == g, r, 0.0)
        parts.append(gu_ref[0, rows, :] * sp)
    oc = jnp.concatenate(parts, axis=0) if len(parts) > 1 else parts[0]
    cat = jnp.concatenate([oa_ref[0], ob, oc], axis=1).astype(BF16)
    y = jnp.dot(cat, wo_ref[...], preferred_element_type=F32)
    o_ref[0] = x_ref[0] + g1_ref[0] * y


def _mix(oa, glu, hist, cw, cb, clg, clb, gu, vsg, wm, bm, wo, x, g1, step, tm):
    bx, tx, _ = x.shape
    hp = hist.shape[1]
    rm = g1.shape[1]
    mod_spec = (pl.BlockSpec((1, 1, D_MODEL), lambda b, i: (b, 0, 0)) if rm == 1 else
                pl.BlockSpec((1, tm, D_MODEL), lambda b, i: (b, i, 0)))
    hist_spec = pl.BlockSpec((1, hp, W_B), lambda b, i: (b, 0, 0))

    def row(w):
        return pl.BlockSpec((1, tm, w), lambda b, i: (b, i, 0))

    return pl.pallas_call(
        functools.partial(_mix_kernel, tm=tm, step=step, hp=hp, carry_tiles=tx > tm),
        grid=(bx, tx // tm),
        in_specs=[row(W_A), row(W_B), hist_spec, _full((32, W_B)), _full((1, W_B)), _full((1, W_B)),
                  _full((1, W_B)), row(W_C), row(W_C), _full((N_GROUPS_C, CHUNK, CHUNK)),
                  _full((CHUNK, W_C)), _full((D_MODEL, D_MODEL)), row(D_MODEL), mod_spec],
        out_specs=[row(D_MODEL), hist_spec],
        out_shape=[jax.ShapeDtypeStruct((bx, tx, D_MODEL), F32), jax.ShapeDtypeStruct((bx, hp, W_B), F32)],
        scratch_shapes=[pltpu.VMEM((hp + tm, W_B), F32)],
        compiler_params=_params(2, VMEM_LIMIT),
        name="mix",
    )(oa, glu, hist, cw, cb, clg, clb, gu, vsg, wm, bm, wo, x, g1)


def _ffn_kernel(x_ref, sh_ref, sc_ref, g2_ref, ng_ref, wg_ref, wu_ref, wd_ref, cw_ref, cb_ref, hist_ref,
                o_ref, tail_ref, xc_ref, *, step, hp, tm, carry_tiles):
    @pl.when(pl.program_id(1) == 0)
    def _():
        xc_ref[0:hp] = hist_ref[0]

    x = x_ref[0]
    y = x * lax.rsqrt(jnp.mean(x * x, axis=-1, keepdims=True) + EPS) * ng_ref[...]
    h = (y * (1.0 + sc_ref[0]) + sh_ref[0]).astype(BF16)
    gate = jnp.dot(h, wg_ref[...], preferred_element_type=F32)
    xc_ref[hp:hp + tm] = gate
    up = jnp.dot(h, wu_ref[...], preferred_element_type=F32)
    gc = (cw_ref[0:1, :] * xc_ref[hp - 2 * step:hp - 2 * step + tm, :]
          + cw_ref[1:2, :] * xc_ref[hp - step:hp - step + tm, :]
          + cw_ref[2:3, :] * gate + cb_ref[...])
    act = (gc * _sigmoid(gc) * up).astype(BF16)
    o_ref[0] = x + g2_ref[0] * jnp.dot(act, wd_ref[...], preferred_element_type=F32)
    tail = xc_ref[tm:tm + hp]
    tail_ref[0] = tail
    if carry_tiles:
        xc_ref[0:hp] = tail


def _ffn(x, sh, sc, g2, ng, wg, wu, wd, cw, cb, hist, step, tm):
    bx, tx, _ = x.shape
    hp = hist.shape[1]
    rm = sh.shape[1]
    mod_spec = (pl.BlockSpec((1, 1, D_MODEL), lambda b, i: (b, 0, 0)) if rm == 1 else
                pl.BlockSpec((1, tm, D_MODEL), lambda b, i: (b, i, 0)))
    row = pl.BlockSpec((1, tm, D_MODEL), lambda b, i: (b, i, 0))
    return pl.pallas_call(
        functools.partial(_ffn_kernel, step=step, hp=hp, tm=tm, carry_tiles=tx > tm),
        grid=(bx, tx // tm),
        in_specs=[row, mod_spec, mod_spec, mod_spec, _full((1, D_MODEL)),
                  _full((D_MODEL, D_FF)), _full((D_MODEL, D_FF)), _full((D_FF, D_MODEL)),
                  _full((8, D_FF)), _full((1, D_FF)),
                  pl.BlockSpec((1, hp, D_FF), lambda b, i: (b, 0, 0))],
        out_specs=[row, pl.BlockSpec((1, hp, D_FF), lambda b, i: (b, 0, 0))],
        out_shape=[jax.ShapeDtypeStruct((bx, tx, D_MODEL), F32), jax.ShapeDtypeStruct((bx, hp, D_FF), F32)],
        scratch_shapes=[pltpu.VMEM((hp + tm, D_FF), F32)],
        compiler_params=_params(2, VMEM_LIMIT),
        name="ffn",
    )(x, sh, sc, g2, ng, wg, wu, wd, cw, cb, hist)


def kernel(x_prompt, x_sample, cache_k, cache_v, state_conv, state_ffn, page_table, c_prompt, c_sample,
           ada_w, ada_b, norm1_g, norm2_g, w_in, q_norm_g, k_norm_g, sb_bias, conv_w, conv_b, conv_ln_g,
           conv_ln_b, sg_ln_g, sg_ln_b, sg_w, sg_b, w_o, ffn_wg, ffn_wu, ffn_conv_w, ffn_conv_b, ffn_wd):
    nb, seq, _ = x_prompt.shape
    db, dt, _ = x_sample.shape
    n_rows = db * dt
    n_pool = cache_k.shape[1]

    c_all = jnp.concatenate([c_prompt, c_sample], axis=0)
    c_pad = (-c_all.shape[0]) % 8
    c_all = jnp.pad(c_all, ((0, c_pad), (0, 0)))
    mod = _modulation(c_all, ada_w, ada_b)

    neg_ss = _neg_suffix_matrix()
    h_of = jnp.arange(W_A) // HEAD_DIM
    gmat = jnp.where(h_of[:, None] == h_of[None, :], 1.0 / HEAD_DIM, 0.0).astype(BF16)
    eye_h = jnp.eye(N_HEADS, dtype=bool)
    tri = jnp.tril(jnp.ones((CHUNK, CHUNK), bool))

    kt_pool = jnp.transpose(cache_k, (0, 1, 3, 4, 2)).reshape(DEPTH, n_pool, W_A, PAGE)
    vt_pool = jnp.transpose(cache_v, (0, 1, 3, 4, 2)).reshape(DEPTH, n_pool, W_A, PAGE)

    xp = x_prompt
    xs = jnp.transpose(x_sample, (1, 0, 2)).reshape(1, n_rows, D_MODEL)
    outs = {n: [] for n in ("kp", "vp", "ks", "vs", "cp", "cs", "fp", "fs", "gs")}
    zeros_conv = jnp.zeros((nb, 32, W_B), F32)
    zeros_ffn = jnp.zeros((nb, 8, D_FF), F32)

    def to_seq(a):
        return jnp.transpose(a.reshape(dt, db, a.shape[-1]), (1, 0, 2))

    def to_tm(a):
        return jnp.transpose(a, (1, 0, 2)).reshape(1, n_rows, a.shape[-1])

    for l in range(DEPTH):
        mod_p = [m[:, None, :] for m in jnp.split(mod[l, :nb], 6, axis=-1)]
        mod_s = [jnp.tile(m, (dt, 1))[None] for m in jnp.split(mod[l, nb:nb + db], 6, axis=-1)]
        w_in_b = w_in[l].astype(BF16)
        wo_b = w_o[l].astype(BF16)
        wg_b, wu_b, wd_b = ffn_wg[l].astype(BF16), ffn_wu[l].astype(BF16), ffn_wd[l].astype(BF16)
        qg = jnp.tile(q_norm_g[l], N_HEADS)[None]
        kg = jnp.tile(k_norm_g[l], N_HEADS)[None]
        n1, n2 = norm1_g[l][None], norm2_g[l][None]
        lng, lnb = sg_ln_g[l][None], sg_ln_b[l][None]
        cw = jnp.pad(conv_w[l], ((0, 1), (0, 0)))
        cb, clg, clb = conv_b[l][None], conv_ln_g[l][None], conv_ln_b[l][None]
        fcw = jnp.pad(ffn_conv_w[l], ((0, 5), (0, 0)))
        fcb = ffn_conv_b[l][None]

        sh1, sc1, g1, sh2, sc2, g2 = mod_p
        q, kf, vf, kb, vb, glu, gu, vsg = _pre(xp, sh1, sc1, n1, w_in_b, qg, kg, gmat, lng, lnb, tm=512)
        oa = _attn_prompt(q, _grouped_kt(kb, PROMPT_GK), vb, _bf16_pieces(sb_bias[l]), neg_ss, PROMPT_GK,
                          PROMPT_TQ)
        wm = jnp.where(tri[None], sg_w[l], 0.0).astype(BF16)
        bm = jnp.repeat(sg_b[l].T, HEAD_DIM, axis=1)
        xp, ctail = _mix(oa, glu, zeros_conv, cw, cb, clg, clb, gu, vsg, wm, bm, wo_b, xp, g1, step=1, tm=512)
        xp, ftail = _ffn(xp, sh2, sc2, g2, n2, wg_b, wu_b, wd_b, fcw, fcb, zeros_ffn, step=1, tm=256)
        outs["kp"].append(kf.reshape(nb, seq, N_HEADS, HEAD_DIM))
        outs["vp"].append(vf.reshape(nb, seq, N_HEADS, HEAD_DIM))
        outs["cp"].append(ctail[:, 32 - (CONV_W - 1):])
        outs["fp"].append(ftail[:, 8 - (FFN_CONV_W - 1):])

        sh1, sc1, g1, sh2, sc2, g2 = mod_s
        q, kf, vf, kb, vb, glu, gu, vsg = _pre(xs, sh1, sc1, n1, w_in_b, qg, kg, gmat, lng, lnb, tm=n_rows)
        q_seq = to_seq(q).reshape(db, dt, N_HEADS, HEAD_DIM)
        qbd = jnp.where(eye_h[None, None, :, :, None], q_seq[:, :, None, :, :],
                        jnp.zeros((), BF16)).reshape(db, dt * N_HEADS, W_A)
        bias_rows = jnp.broadcast_to(jnp.tile(sb_bias[l], dt)[:, None], (dt * N_HEADS, LANES))
        k_new = jnp.pad(to_seq(kb).astype(F32), ((0, 0), (0, 8 - dt), (0, 0)))
        v_new = jnp.pad(to_seq(vb).astype(F32), ((0, 0), (0, 8 - dt), (0, 0)))
        oa = to_tm(_attn_sample(qbd, bias_rows, k_new, v_new, neg_ss, kt_pool, vt_pool, page_table, l,
                                SAMPLE_PAGES_PER_STEP))
        hist = jnp.transpose(state_conv[l], (1, 0, 2)).reshape(1, (CONV_W - 1) * db, W_B)
        w4 = jnp.where(tri[:dt, :dt][None], sg_w[l][:, :dt, :dt], 0.0)
        wm = jnp.einsum("gts,bc->gtbsc", w4, jnp.eye(db, dtype=F32)).reshape(N_GROUPS_C, n_rows, n_rows)
        bm = jnp.repeat(jnp.repeat(sg_b[l][:, :dt].T, db, axis=0), HEAD_DIM, axis=1)
        xs, ctail = _mix(oa, glu, hist, cw, cb, clg, clb, gu, vsg, wm.astype(BF16), bm, wo_b, xs, g1,
                         step=db, tm=n_rows)
        fhist = jnp.transpose(state_ffn[l], (1, 0, 2)).reshape(1, (FFN_CONV_W - 1) * db, D_FF)
        xs, ftail = _ffn(xs, sh2, sc2, g2, n2, wg_b, wu_b, wd_b, fcw, fcb, fhist, step=db, tm=n_rows)
        outs["ks"].append(to_seq(kf).reshape(db, dt, N_HEADS, HEAD_DIM))
        outs["vs"].append(to_seq(vf).reshape(db, dt, N_HEADS, HEAD_DIM))
        outs["cs"].append(jnp.transpose(ctail.reshape(CONV_W - 1, db, W_B), (1, 0, 2)))
        outs["fs"].append(jnp.transpose(ftail.reshape(FFN_CONV_W - 1, db, D_FF), (1, 0, 2)))
        outs["gs"].append(to_seq(vsg))

    ys = to_seq(xs)
    st = {n: jnp.stack(v) for n, v in outs.items()}
    return (xp, ys, st["kp"], st["vp"], st["ks"], st["vs"], st["cp"], st["cs"], st["fp"], st["fs"], st["gs"])
```

```python
import functools

import jax
import jax.numpy as jnp
from jax import lax
from jax.experimental import pallas as pl
from jax.experimental.pallas import tpu as pltpu

F32 = jnp.float32
BF16 = jnp.bfloat16

D_MODEL = 1024
DEPTH = 2
HEAD_DIM = 64
W_A = 512
N_HEADS = 8
W_B = 256
W_C = 256
N_GROUPS_C = 4
CONV_W = 31
CHUNK = 128
FFN_CONV_W = 3
D_FF = 2816
IN_WIDTH = 3 * W_A + 2 * W_B + 2 * W_C
EPS = 1e-6
PAGE = 128

MASKED_LOG_WEIGHT = -1e30
PROMPT_GK = 4
PROMPT_TQ = 256
SAMPLE_PAGES_PER_STEP = 32

LANES = 128
VMEM_LIMIT = 56 * 1024 * 1024


def _params(n_axes, vmem=None):
    return pltpu.CompilerParams(dimension_semantics=("arbitrary",) * n_axes,
                                vmem_limit_bytes=vmem)


def _full(shape):
    return pl.BlockSpec(shape, lambda *_: (0,) * len(shape))


def _sigmoid(x):
    return 1.0 / (1.0 + jnp.exp(-x))


def _gelu_tanh(x):
    c = 0.7978845608028654
    return x * (0.5 * (1.0 + jnp.tanh(c * (x + 0.044715 * (x * x * x)))))


def _layer_norm(x, g, b):
    xc = x - jnp.mean(x, axis=-1, keepdims=True)
    return xc * lax.rsqrt(jnp.mean(xc * xc, axis=-1, keepdims=True) + EPS) * g + b


def _softplus(z):
    neg_abs = lax.bitcast_convert_type(
        lax.bitcast_convert_type(z, jnp.uint32) | jnp.uint32(0x80000000), F32)
    return jnp.maximum(z, 0.0) + jnp.log(1.0 + jnp.exp(neg_abs))


def _neg_suffix_matrix():
    r = jnp.arange(2 * LANES)
    later = (r[:, None] % LANES) > r[None, :]
    return -jnp.where(r[None, :] < LANES, later, True).astype(BF16)


def _stick_block(z, neg_ss, mask=None, split=True):
    sp = _softplus(z)
    if mask is not None:
        sp = jnp.where(mask, sp, 0.0)
    hi = sp.astype(BF16)
    if split:
        lo = (sp - hi.astype(F32)).astype(BF16)
        res = jnp.dot(jnp.concatenate([hi, lo], axis=1), neg_ss, preferred_element_type=F32)
    else:
        res = jnp.dot(hi, neg_ss[:LANES], preferred_element_type=F32)
    return z - sp + res[:, :LANES], res[:, LANES:]


def _stick_weights(z, carry, neg_ss, mask=None):
    arg, tot = _stick_block(z, neg_ss, mask)
    w = jnp.exp(arg + carry)
    if mask is not None:
        w = jnp.where(mask, w, 0.0)
    return w.astype(BF16), carry + tot


def _mod_kernel(c_ref, w_ref, b_ref, o_ref):
    c = c_ref[...]
    s = (c * _sigmoid(c)).astype(BF16)
    o_ref[0] = jnp.dot(s, w_ref[0].astype(BF16), preferred_element_type=F32) + b_ref[0]


def _modulation(c_all, ada_w, ada_b):
    rows = c_all.shape[0]
    tn = 1536
    return pl.pallas_call(
        _mod_kernel,
        grid=(DEPTH, 6 * D_MODEL // tn),
        in_specs=[_full((rows, D_MODEL)),
                  pl.BlockSpec((1, D_MODEL, tn), lambda l, n: (l, 0, n)),
                  pl.BlockSpec((1, 1, tn), lambda l, n: (l, 0, n))],
        out_specs=pl.BlockSpec((1, rows, tn), lambda l, n: (l, 0, n)),
        out_shape=jax.ShapeDtypeStruct((DEPTH, rows, 6 * D_MODEL), F32),
        compiler_params=_params(2, VMEM_LIMIT),
        name="modulation",
    )(c_all, ada_w, ada_b.reshape(DEPTH, 1, 6 * D_MODEL))


def _pre_kernel(x_ref, sh_ref, sc_ref, g_ref, win_ref, qg_ref, kg_ref, gmat_ref, lng_ref, lnb_ref,
                q_ref, kf_ref, vf_ref, kb_ref, vb_ref, glu_ref, gu_ref, vsg_ref):
    x = x_ref[0]
    y = x * lax.rsqrt(jnp.mean(x * x, axis=-1, keepdims=True) + EPS) * g_ref[...]
    h = (y * (1.0 + sc_ref[0]) + sh_ref[0]).astype(BF16)

    def proj(lo, hi):
        return jnp.dot(h, win_ref[:, lo:hi], preferred_element_type=F32)

    def head_norm(t, g):
        ms = jnp.dot((t * t).astype(BF16), gmat_ref[...], preferred_element_type=F32)
        return t * lax.rsqrt(ms + EPS) * g

    q = head_norm(proj(0, W_A), qg_ref[...])
    q_ref[0] = (q * (HEAD_DIM ** -0.5)).astype(BF16)
    k = head_norm(proj(W_A, 2 * W_A), kg_ref[...])
    kf_ref[0] = k
    kb_ref[0] = k.astype(BF16)
    v = proj(2 * W_A, 3 * W_A)
    vf_ref[0] = v
    vb_ref[0] = v.astype(BF16)
    o = 3 * W_A
    glu_ref[0] = proj(o, o + W_B) * _sigmoid(proj(o + W_B, o + 2 * W_B))
    o += 2 * W_B
    gu_ref[0] = _gelu_tanh(proj(o, o + W_C))
    vsg_ref[0] = _layer_norm(_gelu_tanh(proj(o + W_C, o + 2 * W_C)), lng_ref[...], lnb_ref[...])


def _pre(x, sh, sc, norm_g, w_in, qg, kg, gmat, lng, lnb, tm):
    bx, tx, _ = x.shape
    rm = sh.shape[1]
    mod_spec = (pl.BlockSpec((1, 1, D_MODEL), lambda b, i: (b, 0, 0)) if rm == 1 else
                pl.BlockSpec((1, tm, D_MODEL), lambda b, i: (b, i, 0)))

    def row(w):
        return pl.BlockSpec((1, tm, w), lambda b, i: (b, i, 0))

    widths = (W_A, W_A, W_A, W_A, W_A, W_B, W_C, W_C)
    dtypes = (BF16, F32, F32, BF16, BF16, F32, F32, F32)
    return pl.pallas_call(
        _pre_kernel,
        grid=(bx, tx // tm),
        in_specs=[row(D_MODEL), mod_spec, mod_spec, _full((1, D_MODEL)), _full((D_MODEL, IN_WIDTH)),
                  _full((1, W_A)), _full((1, W_A)), _full((W_A, W_A)), _full((1, W_C)), _full((1, W_C))],
        out_specs=[row(w) for w in widths],
        out_shape=[jax.ShapeDtypeStruct((bx, tx, w), d) for w, d in zip(widths, dtypes)],
        compiler_params=_params(2, VMEM_LIMIT),
        name="pre",
    )(x, sh, sc, norm_g, w_in, qg, kg, gmat, lng, lnb)


def _attn_prompt_kernel(bias_ref, q_ref, kt_ref, v_ref, ss_ref, o_ref, w_ref, z_ref, tot_ref, carry_ref, acc_ref,
                        *, gk, tq):
    p = pl.program_id(1)
    i = pl.program_id(2)
    gkeys = gk * LANES
    q = q_ref[0].astype(F32)
    lane = lax.broadcasted_iota(jnp.int32, (tq, LANES), 1)
    first = lane < HEAD_DIM
    neg_ss = ss_ref[...]
    g_diag = (i * tq + tq - 1) // gkeys

    def query(h):
        qh = jnp.where(first, q, 0.0) if h == 0 else jnp.where(first, 0.0, q)
        b = [bias_ref[(2 * p + h) * 3 + j] for j in range(3)]
        off = jnp.where(lane == 0, b[0], jnp.where(lane == 1, b[1], jnp.where(lane == 2, b[2], 0.0)))
        return jnp.concatenate([qh, off], axis=1).astype(BF16)

    qs = (query(0), query(1))

    half = gkeys // 2

    def trip(zs, mask, g_scores, g_weigh):
        kt = kt_ref[0, 0, g_scores]
        nxt = [[None, None], [None, None]]
        pv = [[None, None], [None, None]]
        def score_half(h, c):
            nxt[h][c] = jnp.dot(qs[h], kt[:, c * half:(c + 1) * half], preferred_element_type=F32)

        fillers = [functools.partial(score_half, h, c) for c in (1, 0) for h in range(2)]
        if g_weigh is not None:
            v = v_ref[0, pl.ds(pl.multiple_of(g_weigh * gkeys, gkeys), gkeys), :]
            wp = (w_ref[0], w_ref[1])

            def weigh_half(h, c):
                keys = slice(c * half, (c + 1) * half)
                pv[h][c] = jnp.dot(wp[h][:, keys], v[keys], preferred_element_type=F32)

            fillers += [functools.partial(weigh_half, h, c) for c in (1, 0) for h in range(2)]

        run = [None, None]
        for blk in reversed(range(gk)):
            cols = slice(blk * LANES, (blk + 1) * LANES)
            m = None if mask is None else mask[:, cols]
            for h in range(2):
                if fillers:
                    fillers.pop(0)()
                a, tot = _stick_block(zs[h][:, cols], neg_ss, m, split=False)
                if m is not None:
                    a = jnp.where(m, a, MASKED_LOG_WEIGHT)
                w_ref[h, :, cols] = jnp.exp(a if run[h] is None else a + run[h]).astype(BF16)
                run[h] = tot if run[h] is None else run[h] + tot
        while fillers:
            fillers.pop(0)()
        for h in range(2):
            if g_weigh is not None:
                carry = carry_ref[h]
                acc_ref[h] += jnp.exp(carry) * (pv[h][0] + pv[h][1])
                carry_ref[h] = carry + tot_ref[h]
            tot_ref[h] = run[h]
            z_ref[h] = jnp.concatenate(nxt[h], axis=1)

    kpos = g_diag * gkeys + lax.broadcasted_iota(jnp.int32, (tq, gkeys), 1)
    qpos = i * tq + lax.broadcasted_iota(jnp.int32, (tq, gkeys), 0)
    carry_ref[...] = jnp.zeros_like(carry_ref)
    acc_ref[...] = jnp.zeros_like(acc_ref)
    kt_diag = kt_ref[0, 0, g_diag]
    trip(tuple(jnp.dot(qs[h], kt_diag, preferred_element_type=F32) for h in range(2)), kpos < qpos,
         jnp.maximum(g_diag - 1, 0), None)

    @pl.loop(0, g_diag)
    def _(it):
        g = g_diag - 1 - it
        trip((z_ref[0], z_ref[1]), None, jnp.maximum(g - 1, 0), g + 1)

    v = v_ref[0, 0:gkeys, :]
    for h in range(2):
        acc_ref[h] += jnp.exp(carry_ref[h]) * jnp.dot(w_ref[h], v, preferred_element_type=F32)
    o_ref[0] = jnp.where(first, acc_ref[0], acc_ref[1])


def _grouped_kt(kb, gk):
    b, t, _ = kb.shape
    gkeys = gk * LANES
    kt = jnp.transpose(kb.reshape(b, t // gkeys, gkeys, W_A // LANES, LANES), (0, 3, 1, 4, 2))
    ones_rows = (jnp.arange(LANES) < 3).astype(BF16)[:, None]
    return jnp.concatenate([kt, jnp.broadcast_to(ones_rows, kt.shape[:3] + (LANES, gkeys))], axis=3)


def _bf16_pieces(x):
    hi = x.astype(BF16).astype(F32)
    mid = (x - hi).astype(BF16).astype(F32)
    lo = (x - hi - mid).astype(BF16).astype(F32)
    return jnp.stack([hi, mid, lo], axis=-1).reshape(-1)


def _attn_prompt(q, kt, v, bias, neg_ss, gk, tq):
    b, t, _ = q.shape
    gkeys = gk * LANES
    grid_spec = pltpu.PrefetchScalarGridSpec(
        num_scalar_prefetch=1,
        grid=(b, W_A // LANES, t // tq),
        in_specs=[pl.BlockSpec((1, tq, LANES), lambda b, p, i, s: (b, i, p)),
                  pl.BlockSpec((1, 1, t // gkeys, 2 * LANES, gkeys), lambda b, p, i, s: (b, p, 0, 0, 0)),
                  pl.BlockSpec((1, t, LANES), lambda b, p, i, s: (b, 0, p)),
                  pl.BlockSpec((2 * LANES, 2 * LANES), lambda b, p, i, s: (0, 0))],
        out_specs=pl.BlockSpec((1, tq, LANES), lambda b, p, i, s: (b, i, p)),
        scratch_shapes=[pltpu.VMEM((2, tq, gkeys), BF16), pltpu.VMEM((2, tq, gkeys), F32)]
                       + [pltpu.VMEM((2, tq, LANES), F32)] * 3,
    )
    return pl.pallas_call(
        functools.partial(_attn_prompt_kernel, gk=gk, tq=tq),
        grid_spec=grid_spec,
        out_shape=jax.ShapeDtypeStruct((b, t, W_A), F32),
        compiler_params=_params(3, VMEM_LIMIT),
        name="attn_prompt",
    )(bias, q, kt, v, neg_ss)


def _attn_sample_kernel(pt_ref, qbd_ref, bias_ref, kn_ref, vn_ref, ss_ref, *refs, pp, n_steps):
    del pt_ref
    k_refs, v_refs = refs[:pp], refs[pp:2 * pp]
    o_ref, acc_ref, carry_ref = refs[2 * pp:]
    s = pl.program_id(1)
    qbd = qbd_ref[0]
    bias = bias_ref[...]
    neg_ss = ss_ref[...]
    nrow = qbd.shape[0]

    @pl.when(s == 0)
    def _():
        lane = lax.broadcasted_iota(jnp.int32, (nrow, LANES), 1)
        row = lax.broadcasted_iota(jnp.int32, (nrow, LANES), 0)
        pad = jnp.zeros((PAGE - 8, W_A), F32)
        kn = jnp.concatenate([kn_ref[0], pad], axis=0).astype(BF16)
        vn = jnp.concatenate([vn_ref[0], pad], axis=0).astype(BF16)
        z = lax.dot_general(qbd, kn, (((1,), (1,)), ((), ())), preferred_element_type=F32) + bias
        w, carry0 = _stick_weights(z, jnp.zeros((nrow, LANES), F32), neg_ss, lane < row // N_HEADS)
        acc_ref[...] = jnp.dot(w, vn, preferred_element_type=F32)
        carry_ref[...] = carry0

    kcat = jnp.concatenate([r[0, 0].astype(BF16) for r in k_refs], axis=1)
    vcat = jnp.concatenate([r[0, 0].astype(BF16) for r in v_refs], axis=1)
    z_all = jnp.dot(qbd, kcat, preferred_element_type=F32)
    carry = carry_ref[...]
    ws = []
    for p in range(pp):
        w, carry = _stick_weights(z_all[:, p * LANES:(p + 1) * LANES] + bias, carry, neg_ss)
        ws.append(w)
    carry_ref[...] = carry
    w_all = jnp.concatenate(ws, axis=1)
    acc = acc_ref[...] + lax.dot_general(w_all, vcat, (((1,), (1,)), ((), ())), preferred_element_type=F32)
    acc_ref[...] = acc

    @pl.when(s == n_steps - 1)
    def _():
        rows = lax.broadcasted_iota(jnp.int32, acc.shape, 0)
        cols = lax.broadcasted_iota(jnp.int32, acc.shape, 1)
        own = jnp.where(rows % N_HEADS == cols // HEAD_DIM, acc, 0.0)
        o_ref[0] = jnp.sum(own.reshape(nrow // N_HEADS, N_HEADS, W_A), axis=1)


def _attn_sample(qbd, bias_rows, k_new, v_new, neg_ss, kt_pool, vt_pool, page_table, layer, pp):
    nb, nrow, _ = qbd.shape
    n_pages = page_table.shape[1]
    n_steps = n_pages // pp

    def page_spec(p):
        return pl.BlockSpec(
            (1, 1, W_A, PAGE),
            lambda b, s, pt: (layer, pt[b, n_pages - 1 - s * pp - p], 0, 0))

    grid_spec = pltpu.PrefetchScalarGridSpec(
        num_scalar_prefetch=1,
        grid=(nb, n_steps),
        in_specs=[pl.BlockSpec((1, nrow, W_A), lambda b, s, pt: (b, 0, 0)),
                  pl.BlockSpec((nrow, LANES), lambda b, s, pt: (0, 0)),
                  pl.BlockSpec((1, 8, W_A), lambda b, s, pt: (b, 0, 0)),
                  pl.BlockSpec((1, 8, W_A), lambda b, s, pt: (b, 0, 0)),
                  pl.BlockSpec((2 * LANES, 2 * LANES), lambda b, s, pt: (0, 0))]
                 + [page_spec(p) for p in range(pp)] * 2,
        out_specs=pl.BlockSpec((1, nrow // N_HEADS, W_A), lambda b, s, pt: (b, 0, 0)),
        scratch_shapes=[pltpu.VMEM((nrow, W_A), F32), pltpu.VMEM((nrow, LANES), F32)],
    )
    return pl.pallas_call(
        functools.partial(_attn_sample_kernel, pp=pp, n_steps=n_steps),
        grid_spec=grid_spec,
        out_shape=jax.ShapeDtypeStruct((nb, nrow // N_HEADS, W_A), F32),
        compiler_params=_params(2, VMEM_LIMIT),
        name="attn_sample",
    )(page_table, qbd, bias_rows, k_new, v_new, neg_ss, *([kt_pool] * pp), *([vt_pool] * pp))


def _conv_module(glu_ref, hist_ref, w_ref, b_ref, lng_ref, lnb_ref, tail_ref, xc_ref, *, step, hp, tm, carry_tiles):
    @pl.when(pl.program_id(1) == 0)
    def _():
        xc_ref[0:hp] = hist_ref[0]

    xc_ref[hp:hp + tm] = glu_ref[0]
    acc = jnp.broadcast_to(b_ref[...], (tm, W_B))
    base = hp - (CONV_W - 1) * step
    for k in range(CONV_W):
        acc = acc + w_ref[k:k + 1, :] * xc_ref[base + k * step:base + k * step + tm, :]
    y = _layer_norm(acc, lng_ref[...], lnb_ref[...])
    tail = xc_ref[tm:tm + hp]
    tail_ref[0] = tail
    if carry_tiles:
        xc_ref[0:hp] = tail
    return y * _sigmoid(y)


def _mix_kernel(oa_ref, glu_ref, hist_ref, cw_ref, cb_ref, clg_ref, clb_ref, gu_ref, vsg_ref, wm_ref, bm_ref,
                wo_ref, x_ref, g1_ref, o_ref, tail_ref, xc_ref, *, tm, step, hp, carry_tiles):
    ob = _conv_module(glu_ref, hist_ref, cw_ref, cb_ref, clg_ref, clb_ref, tail_ref, xc_ref,
                      step=step, hp=hp, tm=tm, carry_tiles=carry_tiles)
    lane = lax.broadcasted_iota(jnp.int32, (CHUNK, W_C), 1)
    parts = []
    for c in range(tm // CHUNK):
        rows = slice(c * CHUNK, (c + 1) * CHUNK)
        vs = vsg_ref[0, rows, :].astype(BF16)
        sp = bm_ref[...]
        for g in range(N_GROUPS_C):
            r = jnp.dot(wm_ref[g], vs, preferred_element_type=F32)
            sp = sp + jnp.where(lane // HEAD_DIM == g, r, 0.0)
        parts.append(gu_ref[0, rows, :] * sp)
    oc = jnp.concatenate(parts, axis=0) if len(parts) > 1 else parts[0]
    cat = jnp.concatenate([oa_ref[0], ob, oc], axis=1).astype(BF16)
    y = jnp.dot(cat, wo_ref[...], preferred_element_type=F32)
    o_ref[0] = x_ref[0] + g1_ref[0] * y


def _mix(oa, glu, hist, cw, cb, clg, clb, gu, vsg, wm, bm, wo, x, g1, step, tm):
    bx, tx, _ = x.shape
    hp = hist.shape[1]
    rm = g1.shape[1]
    mod_spec = (pl.BlockSpec((1, 1, D_MODEL), lambda b, i: (b, 0, 0)) if rm == 1 else
                pl.BlockSpec((1, tm, D_MODEL), lambda b, i: (b, i, 0)))
    hist_spec = pl.BlockSpec((1, hp, W_B), lambda b, i: (b, 0, 0))

    def row(w):
        return pl.BlockSpec((1, tm, w), lambda b, i: (b, i, 0))

    return pl.pallas_call(
        functools.partial(_mix_kernel, tm=tm, step=step, hp=hp, carry_tiles=tx > tm),
        grid=(bx, tx // tm),
        in_specs=[row(W_A), row(W_B), hist_spec, _full((32, W_B)), _full((1, W_B)), _full((1, W_B)),
                  _full((1, W_B)), row(W_C), row(W_C), _full((N_GROUPS_C, CHUNK, CHUNK)),
                  _full((CHUNK, W_C)), _full((D_MODEL, D_MODEL)), row(D_MODEL), mod_spec],
        out_specs=[row(D_MODEL), hist_spec],
        out_shape=[jax.ShapeDtypeStruct((bx, tx, D_MODEL), F32), jax.ShapeDtypeStruct((bx, hp, W_B), F32)],
        scratch_shapes=[pltpu.VMEM((hp + tm, W_B), F32)],
        compiler_params=_params(2, VMEM_LIMIT),
        name="mix",
    )(oa, glu, hist, cw, cb, clg, clb, gu, vsg, wm, bm, wo, x, g1)


def _ffn_kernel(x_ref, sh_ref, sc_ref, g2_ref, ng_ref, wg_ref, wu_ref, wd_ref, cw_ref, cb_ref, hist_ref,
                o_ref, tail_ref, xc_ref, *, step, hp, tm, carry_tiles):
    @pl.when(pl.program_id(1) == 0)
    def _():
        xc_ref[0:hp] = hist_ref[0]

    x = x_ref[0]
    y = x * lax.rsqrt(jnp.mean(x * x, axis=-1, keepdims=True) + EPS) * ng_ref[...]
    h = (y * (1.0 + sc_ref[0]) + sh_ref[0]).astype(BF16)
    gate = jnp.dot(h, wg_ref[...], preferred_element_type=F32)
    xc_ref[hp:hp + tm] = gate
    up = jnp.dot(h, wu_ref[...], preferred_element_type=F32)
    gc = (cw_ref[0:1, :] * xc_ref[hp - 2 * step:hp - 2 * step + tm, :]
          + cw_ref[1:2, :] * xc_ref[hp - step:hp - step + tm, :]
          + cw_ref[2:3, :] * gate + cb_ref[...])
    act = (gc * _sigmoid(gc) * up).astype(BF16)
    o_ref[0] = x + g2_ref[0] * jnp.dot(act, wd_ref[...], preferred_element_type=F32)
    tail = xc_ref[tm:tm + hp]
    tail_ref[0] = tail
    if carry_tiles:
        xc_ref[0:hp] = tail


def _ffn(x, sh, sc, g2, ng, wg, wu, wd, cw, cb, hist, step, tm):
    bx, tx, _ = x.shape
    hp = hist.shape[1]
    rm = sh.shape[1]
    mod_spec = (pl.BlockSpec((1, 1, D_MODEL), lambda b, i: (b, 0, 0)) if rm == 1 else
                pl.BlockSpec((1, tm, D_MODEL), lambda b, i: (b, i, 0)))
    row = pl.BlockSpec((1, tm, D_MODEL), lambda b, i: (b, i, 0))
    return pl.pallas_call(
        functools.partial(_ffn_kernel, step=step, hp=hp, tm=tm, carry_tiles=tx > tm),
        grid=(bx, tx // tm),
        in_specs=[row, mod_spec, mod_spec, mod_spec, _full((1, D_MODEL)),
                  _full((D_MODEL, D_FF)), _full((D_MODEL, D_FF)), _full((D_FF, D_MODEL)),
                  _full((8, D_FF)), _full((1, D_FF)),
                  pl.BlockSpec((1, hp, D_FF), lambda b, i: (b, 0, 0))],
        out_specs=[row, pl.BlockSpec((1, hp, D_FF), lambda b, i: (b, 0, 0))],
        out_shape=[jax.ShapeDtypeStruct((bx, tx, D_MODEL), F32), jax.ShapeDtypeStruct((bx, hp, D_FF), F32)],
        scratch_shapes=[pltpu.VMEM((hp + tm, D_FF), F32)],
        compiler_params=_params(2, VMEM_LIMIT),
        name="ffn",
    )(x, sh, sc, g2, ng, wg, wu, wd, cw, cb, hist)


def kernel(x_prompt, x_sample, cache_k, cache_v, state_conv, state_ffn, page_table, c_prompt, c_sample,
           ada_w, ada_b, norm1_g, norm2_g, w_in, q_norm_g, k_norm_g, sb_bias, conv_w, conv_b, conv_ln_g,
           conv_ln_b, sg_ln_g, sg_ln_b, sg_w, sg_b, w_o, ffn_wg, ffn_wu, ffn_conv_w, ffn_conv_b, ffn_wd):
    nb, seq, _ = x_prompt.shape
    db, dt, _ = x_sample.shape
    n_rows = db * dt
    n_pool = cache_k.shape[1]

    c_all = jnp.concatenate([c_prompt, c_sample], axis=0)
    c_pad = (-c_all.shape[0]) % 8
    c_all = jnp.pad(c_all, ((0, c_pad), (0, 0)))
    mod = _modulation(c_all, ada_w, ada_b)

    neg_ss = _neg_suffix_matrix()
    h_of = jnp.arange(W_A) // HEAD_DIM
    gmat = jnp.where(h_of[:, None] == h_of[None, :], 1.0 / HEAD_DIM, 0.0).astype(BF16)
    eye_h = jnp.eye(N_HEADS, dtype=bool)
    tri = jnp.tril(jnp.ones((CHUNK, CHUNK), bool))

    kt_pool = jnp.transpose(cache_k, (0, 1, 3, 4, 2)).reshape(DEPTH, n_pool, W_A, PAGE)
    vt_pool = jnp.transpose(cache_v, (0, 1, 3, 4, 2)).reshape(DEPTH, n_pool, W_A, PAGE)

    xp = x_prompt
    xs = jnp.transpose(x_sample, (1, 0, 2)).reshape(1, n_rows, D_MODEL)
    outs = {n: [] for n in ("kp", "vp", "ks", "vs", "cp", "cs", "fp", "fs", "gs")}
    zeros_conv = jnp.zeros((nb, 32, W_B), F32)
    zeros_ffn = jnp.zeros((nb, 8, D_FF), F32)

    def to_seq(a):
        return jnp.transpose(a.reshape(dt, db, a.shape[-1]), (1, 0, 2))

    def to_tm(a):
        return jnp.transpose(a, (1, 0, 2)).reshape(1, n_rows, a.shape[-1])

    for l in range(DEPTH):
        mod_p = [m[:, None, :] for m in jnp.split(mod[l, :nb], 6, axis=-1)]
        mod_s = [jnp.tile(m, (dt, 1))[None] for m in jnp.split(mod[l, nb:nb + db], 6, axis=-1)]
        w_in_b = w_in[l].astype(BF16)
        wo_b = w_o[l].astype(BF16)
        wg_b, wu_b, wd_b = ffn_wg[l].astype(BF16), ffn_wu[l].astype(BF16), ffn_wd[l].astype(BF16)
        qg = jnp.tile(q_norm_g[l], N_HEADS)[None]
        kg = jnp.tile(k_norm_g[l], N_HEADS)[None]
        n1, n2 = norm1_g[l][None], norm2_g[l][None]
        lng, lnb = sg_ln_g[l][None], sg_ln_b[l][None]
        cw = jnp.pad(conv_w[l], ((0, 1), (0, 0)))
        cb, clg, clb = conv_b[l][None], conv_ln_g[l][None], conv_ln_b[l][None]
        fcw = jnp.pad(ffn_conv_w[l], ((0, 5), (0, 0)))
        fcb = ffn_conv_b[l][None]

        sh1, sc1, g1, sh2, sc2, g2 = mod_p
        q, kf, vf, kb, vb, glu, gu, vsg = _pre(xp, sh1, sc1, n1, w_in_b, qg, kg, gmat, lng, lnb, tm=512)
        oa = _attn_prompt(q, _grouped_kt(kb, PROMPT_GK), vb, _bf16_pieces(sb_bias[l]), neg_ss, PROMPT_GK,
                          PROMPT_TQ)
        wm = jnp.where(tri[None], sg_w[l], 0.0).astype(BF16)
        bm = jnp.repeat(sg_b[l].T, HEAD_DIM, axis=1)
        xp, ctail = _mix(oa, glu, zeros_conv, cw, cb, clg, clb, gu, vsg, wm, bm, wo_b, xp, g1, step=1, tm=512)
        xp, ftail = _ffn(xp, sh2, sc2, g2, n2, wg_b, wu_b, wd_b, fcw, fcb, zeros_ffn, step=1, tm=256)
        outs["kp"].append(kf.reshape(nb, seq, N_HEADS, HEAD_DIM))
        outs["vp"].append(vf.reshape(nb, seq, N_HEADS, HEAD_DIM))
        outs["cp"].append(ctail[:, 32 - (CONV_W - 1):])
        outs["fp"].append(ftail[:, 8 - (FFN_CONV_W - 1):])

        sh1, sc1, g1, sh2, sc2, g2 = mod_s
        q, kf, vf, kb, vb, glu, gu, vsg = _pre(xs, sh1, sc1, n1, w_in_b, qg, kg, gmat, lng, lnb, tm=n_rows)
        q_seq = to_seq(q).reshape(db, dt, N_HEADS, HEAD_DIM)
        qbd = jnp.where(eye_h[None, None, :, :, None], q_seq[:, :, None, :, :],
                        jnp.zeros((), BF16)).reshape(db, dt * N_HEADS, W_A)
        bias_rows = jnp.broadcast_to(jnp.tile(sb_bias[l], dt)[:, None], (dt * N_HEADS, LANES))
        k_new = jnp.pad(to_seq(kb).astype(F32), ((0, 0), (0, 8 - dt), (0, 0)))
        v_new = jnp.pad(to_seq(vb).astype(F32), ((0, 0), (0, 8 - dt), (0, 0)))
        oa = to_tm(_attn_sample(qbd, bias_rows, k_new, v_new, neg_ss, kt_pool, vt_pool, page_table, l,
                                SAMPLE_PAGES_PER_STEP))
        hist = jnp.transpose(state_conv[l], (1, 0, 2)).reshape(1, (CONV_W - 1) * db, W_B)
        w4 = jnp.where(tri[:dt, :dt][None], sg_w[l][:, :dt, :dt], 0.0)
        wm = jnp.einsum("gts,bc->gtbsc", w4, jnp.eye(db, dtype=F32)).reshape(N_GROUPS_C, n_rows, n_rows)
        bm = jnp.repeat(jnp.repeat(sg_b[l][:, :dt].T, db, axis=0), HEAD_DIM, axis=1)
        xs, ctail = _mix(oa, glu, hist, cw, cb, clg, clb, gu, vsg, wm.astype(BF16), bm, wo_b, xs, g1,
                         step=db, tm=n_rows)
        fhist = jnp.transpose(state_ffn[l], (1, 0, 2)).reshape(1, (FFN_CONV_W - 1) * db, D_FF)
        xs, ftail = _ffn(xs, sh2, sc2, g2, n2, wg_b, wu_b, wd_b, fcw, fcb, fhist, step=db, tm=n_rows)
        outs["ks"].append(to_seq(kf).reshape(db, dt, N_HEADS, HEAD_DIM))
        outs["vs"].append(to_seq(vf).reshape(db, dt, N_HEADS, HEAD_DIM))
        outs["cs"].append(jnp.transpose(ctail.reshape(CONV_W - 1, db, W_B), (1, 0, 2)))
        outs["fs"].append(jnp.transpose(ftail.reshape(FFN_CONV_W - 1, db, D_FF), (1, 0, 2)))
        outs["gs"].append(to_seq(vsg))

    ys = to_seq(xs)
    st = {n: jnp.stack(v) for n, v in outs.items()}
    return (xp, ys, st["kp"], st["vp"], st["ks"], st["vs"], st["cp"], st["cs"], st["fp"], st["fs"], st["gs"])
```

```python
import functools

import jax
import jax.numpy as jnp
from jax import lax
from jax.experimental import pallas as pl
from jax.experimental.pallas import tpu as pltpu

F32 = jnp.float32
BF16 = jnp.bfloat16

D_MODEL = 1024
DEPTH = 2
HEAD_DIM = 64
W_A = 512
N_HEADS = 8
W_B = 256
W_C = 256
N_GROUPS_C = 4
CONV_W = 31
CHUNK = 128
FFN_CONV_W = 3
D_FF = 2816
IN_WIDTH = 3 * W_A + 2 * W_B + 2 * W_C
EPS = 1e-6
PAGE = 128

MASKED_LOG_WEIGHT = -1e30
PROMPT_GK = 4
PROMPT_TQ = 256
SAMPLE_PAGES_PER_STEP = 32

LANES = 128
VMEM_LIMIT = 56 * 1024 * 1024


def _params(n_axes, vmem=None):
    return pltpu.CompilerParams(dimension_semantics=("arbitrary",) * n_axes,
                                vmem_limit_bytes=vmem)


def _full(shape):
    return pl.BlockSpec(shape, lambda *_: (0,) * len(shape))


def _sigmoid(x):
    return 1.0 / (1.0 + jnp.exp(-x))


def _gelu_tanh(x):
    c = 0.7978845608028654
    return x * (0.5 * (1.0 + jnp.tanh(c * (x + 0.044715 * (x * x * x)))))


def _layer_norm(x, g, b):
    xc = x - jnp.mean(x, axis=-1, keepdims=True)
    return xc * lax.rsqrt(jnp.mean(xc * xc, axis=-1, keepdims=True) + EPS) * g + b


def _softplus(z):
    neg_abs = lax.bitcast_convert_type(
        lax.bitcast_convert_type(z, jnp.uint32) | jnp.uint32(0x80000000), F32)
    return jnp.maximum(z, 0.0) + jnp.log(1.0 + jnp.exp(neg_abs))


def _neg_suffix_matrix():
    r = jnp.arange(2 * LANES)
    later = (r[:, None] % LANES) > r[None, :]
    return -jnp.where(r[None, :] < LANES, later, True).astype(BF16)


def _stick_block(z, neg_ss, mask=None, split=True):
    sp = _softplus(z)
    if mask is not None:
        sp = jnp.where(mask, sp, 0.0)
    hi = sp.astype(BF16)
    if split:
        lo = (sp - hi.astype(F32)).astype(BF16)
        res = jnp.dot(jnp.concatenate([hi, lo], axis=1), neg_ss, preferred_element_type=F32)
    else:
        res = jnp.dot(hi, neg_ss[:LANES], preferred_element_type=F32)
    return z - sp + res[:, :LANES], res[:, LANES:]


def _stick_weights(z, carry, neg_ss, mask=None):
    arg, tot = _stick_block(z, neg_ss, mask)
    w = jnp.exp(arg + carry)
    if mask is not None:
        w = jnp.where(mask, w, 0.0)
    return w.astype(BF16), carry + tot


def _mod_kernel(c_ref, w_ref, b_ref, o_ref):
    c = c_ref[...]
    s = (c * _sigmoid(c)).astype(BF16)
    o_ref[0] = jnp.dot(s, w_ref[0].astype(BF16), preferred_element_type=F32) + b_ref[0]


def _modulation(c_all, ada_w, ada_b):
    rows = c_all.shape[0]
    tn = 1536
    return pl.pallas_call(
        _mod_kernel,
        grid=(DEPTH, 6 * D_MODEL // tn),
        in_specs=[_full((rows, D_MODEL)),
                  pl.BlockSpec((1, D_MODEL, tn), lambda l, n: (l, 0, n)),
                  pl.BlockSpec((1, 1, tn), lambda l, n: (l, 0, n))],
        out_specs=pl.BlockSpec((1, rows, tn), lambda l, n: (l, 0, n)),
        out_shape=jax.ShapeDtypeStruct((DEPTH, rows, 6 * D_MODEL), F32),
        compiler_params=_params(2, VMEM_LIMIT),
        name="modulation",
    )(c_all, ada_w, ada_b.reshape(DEPTH, 1, 6 * D_MODEL))


def _pre_kernel(x_ref, sh_ref, sc_ref, g_ref, win_ref, qg_ref, kg_ref, gmat_ref, lng_ref, lnb_ref,
                q_ref, kf_ref, vf_ref, kb_ref, vb_ref, glu_ref, gu_ref, vsg_ref):
    x = x_ref[0]
    y = x * lax.rsqrt(jnp.mean(x * x, axis=-1, keepdims=True) + EPS) * g_ref[...]
    h = (y * (1.0 + sc_ref[0]) + sh_ref[0]).astype(BF16)

    def proj(lo, hi):
        return jnp.dot(h, win_ref[:, lo:hi], preferred_element_type=F32)

    def head_norm(t, g):
        ms = jnp.dot((t * t).astype(BF16), gmat_ref[...], preferred_element_type=F32)
        return t * lax.rsqrt(ms + EPS) * g

    q = head_norm(proj(0, W_A), qg_ref[...])
    q_ref[0] = (q * (HEAD_DIM ** -0.5)).astype(BF16)
    k = head_norm(proj(W_A, 2 * W_A), kg_ref[...])
    kf_ref[0] = k
    kb_ref[0] = k.astype(BF16)
    v = proj(2 * W_A, 3 * W_A)
    vf_ref[0] = v
    vb_ref[0] = v.astype(BF16)
    o = 3 * W_A
    glu_ref[0] = proj(o, o + W_B) * _sigmoid(proj(o + W_B, o + 2 * W_B))
    o += 2 * W_B
    gu_ref[0] = _gelu_tanh(proj(o, o + W_C))
    vsg_ref[0] = _layer_norm(_gelu_tanh(proj(o + W_C, o + 2 * W_C)), lng_ref[...], lnb_ref[...])


def _pre(x, sh, sc, norm_g, w_in, qg, kg, gmat, lng, lnb, tm):
    bx, tx, _ = x.shape
    rm = sh.shape[1]
    mod_spec = (pl.BlockSpec((1, 1, D_MODEL), lambda b, i: (b, 0, 0)) if rm == 1 else
                pl.BlockSpec((1, tm, D_MODEL), lambda b, i: (b, i, 0)))

    def row(w):
        return pl.BlockSpec((1, tm, w), lambda b, i: (b, i, 0))

    widths = (W_A, W_A, W_A, W_A, W_A, W_B, W_C, W_C)
    dtypes = (BF16, F32, F32, BF16, BF16, F32, F32, F32)
    return pl.pallas_call(
        _pre_kernel,
        grid=(bx, tx // tm),
        in_specs=[row(D_MODEL), mod_spec, mod_spec, _full((1, D_MODEL)), _full((D_MODEL, IN_WIDTH)),
                  _full((1, W_A)), _full((1, W_A)), _full((W_A, W_A)), _full((1, W_C)), _full((1, W_C))],
        out_specs=[row(w) for w in widths],
        out_shape=[jax.ShapeDtypeStruct((bx, tx, w), d) for w, d in zip(widths, dtypes)],
        compiler_params=_params(2, VMEM_LIMIT),
        name="pre",
    )(x, sh, sc, norm_g, w_in, qg, kg, gmat, lng, lnb)


def _attn_prompt_kernel(bias_ref, q_ref, kt_ref, v_ref, ss_ref, o_ref, w_ref, z_ref, tot_ref, carry_ref, acc_ref,
                        *, gk, tq):
    p = pl.program_id(1)
    i = pl.program_id(2)
    gkeys = gk * LANES
    q = q_ref[0].astype(F32)
    lane = lax.broadcasted_iota(jnp.int32, (tq, LANES), 1)
    first = lane < HEAD_DIM
    neg_ss = ss_ref[...]
    g_diag = (i * tq + tq - 1) // gkeys

    def query(h):
        qh = jnp.where(first, q, 0.0) if h == 0 else jnp.where(first, 0.0, q)
        b = [bias_ref[(2 * p + h) * 3 + j] for j in range(3)]
        off = jnp.where(lane == 0, b[0], jnp.where(lane == 1, b[1], jnp.where(lane == 2, b[2], 0.0)))
        return jnp.concatenate([qh, off], axis=1).astype(BF16)

    qs = (query(0), query(1))

    half = gkeys // 2

    def trip(zs, mask, g_scores, g_weigh):
        kt = kt_ref[0, 0, g_scores]
        nxt = [[None, None], [None, None]]
        pv = [[None, None], [None, None]]
        fillers = []
        if g_weigh is not None:
            v = v_ref[0, pl.ds(pl.multiple_of(g_weigh * gkeys, gkeys), gkeys), :]
            wp = (w_ref[0], w_ref[1])

            def weigh_half(h, c):
                keys = slice(c * half, (c + 1) * half)
                pv[h][c] = jnp.dot(wp[h][:, keys], v[keys], preferred_element_type=F32)

            fillers += [functools.partial(weigh_half, h, c) for c in (1, 0) for h in range(2)]

        def score_half(h, c):
            nxt[h][c] = jnp.dot(qs[h], kt[:, c * half:(c + 1) * half], preferred_element_type=F32)

        fillers += [functools.partial(score_half, h, c) for c in (1, 0) for h in range(2)]

        run = [None, None]
        for blk in reversed(range(gk)):
            cols = slice(blk * LANES, (blk + 1) * LANES)
            m = None if mask is None else mask[:, cols]
            for h in range(2):
                if fillers:
                    fillers.pop(0)()
                a, tot = _stick_block(zs[h][:, cols], neg_ss, m, split=False)
                if m is not None:
                    a = jnp.where(m, a, MASKED_LOG_WEIGHT)
                w_ref[h, :, cols] = jnp.exp(a if run[h] is None else a + run[h]).astype(BF16)
                run[h] = tot if run[h] is None else run[h] + tot
        while fillers:
            fillers.pop(0)()
        for h in range(2):
            if g_weigh is not None:
                carry = carry_ref[h]
                acc_ref[h] += jnp.exp(carry) * (pv[h][0] + pv[h][1])
                carry_ref[h] = carry + tot_ref[h]
            tot_ref[h] = run[h]
            z_ref[h] = jnp.concatenate(nxt[h], axis=1)

    kpos = g_diag * gkeys + lax.broadcasted_iota(jnp.int32, (tq, gkeys), 1)
    qpos = i * tq + lax.broadcasted_iota(jnp.int32, (tq, gkeys), 0)
    carry_ref[...] = jnp.zeros_like(carry_ref)
    acc_ref[...] = jnp.zeros_like(acc_ref)
    kt_diag = kt_ref[0, 0, g_diag]
    trip(tuple(jnp.dot(qs[h], kt_diag, preferred_element_type=F32) for h in range(2)), kpos < qpos,
         jnp.maximum(g_diag - 1, 0), None)

    @pl.loop(0, g_diag)
    def _(it):
        g = g_diag - 1 - it
        trip((z_ref[0], z_ref[1]), None, jnp.maximum(g - 1, 0), g + 1)

    v = v_ref[0, 0:gkeys, :]
    for h in range(2):
        acc_ref[h] += jnp.exp(carry_ref[h]) * jnp.dot(w_ref[h], v, preferred_element_type=F32)
    o_ref[0] = jnp.where(first, acc_ref[0], acc_ref[1])


def _grouped_kt(kb, gk):
    b, t, _ = kb.shape
    gkeys = gk * LANES
    kt = jnp.transpose(kb.reshape(b, t // gkeys, gkeys, W_A // LANES, LANES), (0, 3, 1, 4, 2))
    ones_rows = (jnp.arange(LANES) < 3).astype(BF16)[:, None]
    return jnp.concatenate([kt, jnp.broadcast_to(ones_rows, kt.shape[:3] + (LANES, gkeys))], axis=3)


def _bf16_pieces(x):
    hi = x.astype(BF16).astype(F32)
    mid = (x - hi).astype(BF16).astype(F32)
    lo = (x - hi - mid).astype(BF16).astype(F32)
    return jnp.stack([hi, mid, lo], axis=-1).reshape(-1)


def _attn_prompt(q, kt, v, bias, neg_ss, gk, tq):
    b, t, _ = q.shape
    gkeys = gk * LANES
    grid_spec = pltpu.PrefetchScalarGridSpec(
        num_scalar_prefetch=1,
        grid=(b, W_A // LANES, t // tq),
        in_specs=[pl.BlockSpec((1, tq, LANES), lambda b, p, i, s: (b, i, p)),
                  pl.BlockSpec((1, 1, t // gkeys, 2 * LANES, gkeys), lambda b, p, i, s: (b, p, 0, 0, 0)),
                  pl.BlockSpec((1, t, LANES), lambda b, p, i, s: (b, 0, p)),
                  pl.BlockSpec((2 * LANES, 2 * LANES), lambda b, p, i, s: (0, 0))],
        out_specs=pl.BlockSpec((1, tq, LANES), lambda b, p, i, s: (b, i, p)),
        scratch_shapes=[pltpu.VMEM((2, tq, gkeys), BF16), pltpu.VMEM((2, tq, gkeys), F32)]
                       + [pltpu.VMEM((2, tq, LANES), F32)] * 3,
    )
    return pl.pallas_call(
        functools.partial(_attn_prompt_kernel, gk=gk, tq=tq),
        grid_spec=grid_spec,
        out_shape=jax.ShapeDtypeStruct((b, t, W_A), F32),
        compiler_params=_params(3, VMEM_LIMIT),
        name="attn_prompt",
    )(bias, q, kt, v, neg_ss)


def _attn_sample_kernel(pt_ref, qbd_ref, bias_ref, kn_ref, vn_ref, ss_ref, *refs, pp, n_steps):
    del pt_ref
    k_refs, v_refs = refs[:pp], refs[pp:2 * pp]
    o_ref, acc_ref, carry_ref = refs[2 * pp:]
    s = pl.program_id(1)
    qbd = qbd_ref[0]
    bias = bias_ref[...]
    neg_ss = ss_ref[...]
    nrow = qbd.shape[0]

    @pl.when(s == 0)
    def _():
        lane = lax.broadcasted_iota(jnp.int32, (nrow, LANES), 1)
        row = lax.broadcasted_iota(jnp.int32, (nrow, LANES), 0)
        pad = jnp.zeros((PAGE - 8, W_A), F32)
        kn = jnp.concatenate([kn_ref[0], pad], axis=0).astype(BF16)
        vn = jnp.concatenate([vn_ref[0], pad], axis=0).astype(BF16)
        z = lax.dot_general(qbd, kn, (((1,), (1,)), ((), ())), preferred_element_type=F32) + bias
        w, carry0 = _stick_weights(z, jnp.zeros((nrow, LANES), F32), neg_ss, lane < row // N_HEADS)
        acc_ref[...] = jnp.dot(w, vn, preferred_element_type=F32)
        carry_ref[...] = carry0

    kcat = jnp.concatenate([r[0, 0].astype(BF16) for r in k_refs], axis=1)
    vcat = jnp.concatenate([r[0, 0].astype(BF16) for r in v_refs], axis=1)
    z_all = jnp.dot(qbd, kcat, preferred_element_type=F32)
    carry = carry_ref[...]
    ws = []
    for p in range(pp):
        w, carry = _stick_weights(z_all[:, p * LANES:(p + 1) * LANES] + bias, carry, neg_ss)
        ws.append(w)
    carry_ref[...] = carry
    w_all = jnp.concatenate(ws, axis=1)
    acc = acc_ref[...] + lax.dot_general(w_all, vcat, (((1,), (1,)), ((), ())), preferred_element_type=F32)
    acc_ref[...] = acc

    @pl.when(s == n_steps - 1)
    def _():
        rows = lax.broadcasted_iota(jnp.int32, acc.shape, 0)
        cols = lax.broadcasted_iota(jnp.int32, acc.shape, 1)
        own = jnp.where(rows % N_HEADS == cols // HEAD_DIM, acc, 0.0)
        o_ref[0] = jnp.sum(own.reshape(nrow // N_HEADS, N_HEADS, W_A), axis=1)


def _attn_sample(qbd, bias_rows, k_new, v_new, neg_ss, kt_pool, vt_pool, page_table, layer, pp):
    nb, nrow, _ = qbd.shape
    n_pages = page_table.shape[1]
    n_steps = n_pages // pp

    def page_spec(p):
        return pl.BlockSpec(
            (1, 1, W_A, PAGE),
            lambda b, s, pt: (layer, pt[b, n_pages - 1 - s * pp - p], 0, 0))

    grid_spec = pltpu.PrefetchScalarGridSpec(
        num_scalar_prefetch=1,
        grid=(nb, n_steps),
        in_specs=[pl.BlockSpec((1, nrow, W_A), lambda b, s, pt: (b, 0, 0)),
                  pl.BlockSpec((nrow, LANES), lambda b, s, pt: (0, 0)),
                  pl.BlockSpec((1, 8, W_A), lambda b, s, pt: (b, 0, 0)),
                  pl.BlockSpec((1, 8, W_A), lambda b, s, pt: (b, 0, 0)),
                  pl.BlockSpec((2 * LANES, 2 * LANES), lambda b, s, pt: (0, 0))]
                 + [page_spec(p) for p in range(pp)] * 2,
        out_specs=pl.BlockSpec((1, nrow // N_HEADS, W_A), lambda b, s, pt: (b, 0, 0)),
        scratch_shapes=[pltpu.VMEM((nrow, W_A), F32), pltpu.VMEM((nrow, LANES), F32)],
    )
    return pl.pallas_call(
        functools.partial(_attn_sample_kernel, pp=pp, n_steps=n_steps),
        grid_spec=grid_spec,
        out_shape=jax.ShapeDtypeStruct((nb, nrow // N_HEADS, W_A), F32),
        compiler_params=_params(2, VMEM_LIMIT),
        name="attn_sample",
    )(page_table, qbd, bias_rows, k_new, v_new, neg_ss, *([kt_pool] * pp), *([vt_pool] * pp))


def _conv_module(glu_ref, hist_ref, w_ref, b_ref, lng_ref, lnb_ref, tail_ref, xc_ref, *, step, hp, tm, carry_tiles):
    @pl.when(pl.program_id(1) == 0)
    def _():
        xc_ref[0:hp] = hist_ref[0]

    xc_ref[hp:hp + tm] = glu_ref[0]
    acc = jnp.broadcast_to(b_ref[...], (tm, W_B))
    base = hp - (CONV_W - 1) * step
    for k in range(CONV_W):
        acc = acc + w_ref[k:k + 1, :] * xc_ref[base + k * step:base + k * step + tm, :]
    y = _layer_norm(acc, lng_ref[...], lnb_ref[...])
    tail = xc_ref[tm:tm + hp]
    tail_ref[0] = tail
    if carry_tiles:
        xc_ref[0:hp] = tail
    return y * _sigmoid(y)


def _mix_kernel(oa_ref, glu_ref, hist_ref, cw_ref, cb_ref, clg_ref, clb_ref, gu_ref, vsg_ref, wm_ref, bm_ref,
                wo_ref, x_ref, g1_ref, o_ref, tail_ref, xc_ref, *, tm, step, hp, carry_tiles):
    ob = _conv_module(glu_ref, hist_ref, cw_ref, cb_ref, clg_ref, clb_ref, tail_ref, xc_ref,
                      step=step, hp=hp, tm=tm, carry_tiles=carry_tiles)
    lane = lax.broadcasted_iota(jnp.int32, (CHUNK, W_C), 1)
    parts = []
    for c in range(tm // CHUNK):
        rows = slice(c * CHUNK, (c + 1) * CHUNK)
        vs = vsg_ref[0, rows, :].astype(BF16)
        sp = bm_ref[...]
        for g in range(N_GROUPS_C):
            r = jnp.dot(wm_ref[g], vs, preferred_element_type=F32)
            sp = sp + jnp.where(lane // HEAD_DIM == g, r, 0.0)
        parts.append(gu_ref[0, rows, :] * sp)
    oc = jnp.concatenate(parts, axis=0) if len(parts) > 1 else parts[0]
    cat = jnp.concatenate([oa_ref[0], ob, oc], axis=1).astype(BF16)
    y = jnp.dot(cat, wo_ref[...], preferred_element_type=F32)
    o_ref[0] = x_ref[0] + g1_ref[0] * y


def _mix(oa, glu, hist, cw, cb, clg, clb, gu, vsg, wm, bm, wo, x, g1, step, tm):
    bx, tx, _ = x.shape
    hp = hist.shape[1]
    rm = g1.shape[1]
    mod_spec = (pl.BlockSpec((1, 1, D_MODEL), lambda b, i: (b, 0, 0)) if rm == 1 else
                pl.BlockSpec((1, tm, D_MODEL), lambda b, i: (b, i, 0)))
    hist_spec = pl.BlockSpec((1, hp, W_B), lambda b, i: (b, 0, 0))

    def row(w):
        return pl.BlockSpec((1, tm, w), lambda b, i: (b, i, 0))

    return pl.pallas_call(
        functools.partial(_mix_kernel, tm=tm, step=step, hp=hp, carry_tiles=tx > tm),
        grid=(bx, tx // tm),
        in_specs=[row(W_A), row(W_B), hist_spec, _full((32, W_B)), _full((1, W_B)), _full((1, W_B)),
                  _full((1, W_B)), row(W_C), row(W_C), _full((N_GROUPS_C, CHUNK, CHUNK)),
                  _full((CHUNK, W_C)), _full((D_MODEL, D_MODEL)), row(D_MODEL), mod_spec],
        out_specs=[row(D_MODEL), hist_spec],
        out_shape=[jax.ShapeDtypeStruct((bx, tx, D_MODEL), F32), jax.ShapeDtypeStruct((bx, hp, W_B), F32)],
        scratch_shapes=[pltpu.VMEM((hp + tm, W_B), F32)],
        compiler_params=_params(2, VMEM_LIMIT),
        name="mix",
    )(oa, glu, hist, cw, cb, clg, clb, gu, vsg, wm, bm, wo, x, g1)


def _ffn_kernel(x_ref, sh_ref, sc_ref, g2_ref, ng_ref, wg_ref, wu_ref, wd_ref, cw_ref, cb_ref, hist_ref,
                o_ref, tail_ref, xc_ref, *, step, hp, tm, carry_tiles):
    @pl.when(pl.program_id(1) == 0)
    def _():
        xc_ref[0:hp] = hist_ref[0]

    x = x_ref[0]
    y = x * lax.rsqrt(jnp.mean(x * x, axis=-1, keepdims=True) + EPS) * ng_ref[...]
    h = (y * (1.0 + sc_ref[0]) + sh_ref[0]).astype(BF16)
    gate = jnp.dot(h, wg_ref[...], preferred_element_type=F32)
    xc_ref[hp:hp + tm] = gate
    up = jnp.dot(h, wu_ref[...], preferred_element_type=F32)
    gc = (cw_ref[0:1, :] * xc_ref[hp - 2 * step:hp - 2 * step + tm, :]
          + cw_ref[1:2, :] * xc_ref[hp - step:hp - step + tm, :]
          + cw_ref[2:3, :] * gate + cb_ref[...])
    act = (gc * _sigmoid(gc) * up).astype(BF16)
    o_ref[0] = x + g2_ref[0] * jnp.dot(act, wd_ref[...], preferred_element_type=F32)
    tail = xc_ref[tm:tm + hp]
    tail_ref[0] = tail
    if carry_tiles:
        xc_ref[0:hp] = tail


def _ffn(x, sh, sc, g2, ng, wg, wu, wd, cw, cb, hist, step, tm):
    bx, tx, _ = x.shape
    hp = hist.shape[1]
    rm = sh.shape[1]
    mod_spec = (pl.BlockSpec((1, 1, D_MODEL), lambda b, i: (b, 0, 0)) if rm == 1 else
                pl.BlockSpec((1, tm, D_MODEL), lambda b, i: (b, i, 0)))
    row = pl.BlockSpec((1, tm, D_MODEL), lambda b, i: (b, i, 0))
    return pl.pallas_call(
        functools.partial(_ffn_kernel, step=step, hp=hp, tm=tm, carry_tiles=tx > tm),
        grid=(bx, tx // tm),
        in_specs=[row, mod_spec, mod_spec, mod_spec, _full((1, D_MODEL)),
                  _full((D_MODEL, D_FF)), _full((D_MODEL, D_FF)), _full((D_FF, D_MODEL)),
                  _full((8, D_FF)), _full((1, D_FF)),
                  pl.BlockSpec((1, hp, D_FF), lambda b, i: (b, 0, 0))],
        out_specs=[row, pl.BlockSpec((1, hp, D_FF), lambda b, i: (b, 0, 0))],
        out_shape=[jax.ShapeDtypeStruct((bx, tx, D_MODEL), F32), jax.ShapeDtypeStruct((bx, hp, D_FF), F32)],
        scratch_shapes=[pltpu.VMEM((hp + tm, D_FF), F32)],
        compiler_params=_params(2, VMEM_LIMIT),
        name="ffn",
    )(x, sh, sc, g2, ng, wg, wu, wd, cw, cb, hist)


def kernel(x_prompt, x_sample, cache_k, cache_v, state_conv, state_ffn, page_table, c_prompt, c_sample,
           ada_w, ada_b, norm1_g, norm2_g, w_in, q_norm_g, k_norm_g, sb_bias, conv_w, conv_b, conv_ln_g,
           conv_ln_b, sg_ln_g, sg_ln_b, sg_w, sg_b, w_o, ffn_wg, ffn_wu, ffn_conv_w, ffn_conv_b, ffn_wd):
    nb, seq, _ = x_prompt.shape
    db, dt, _ = x_sample.shape
    n_rows = db * dt
    n_pool = cache_k.shape[1]

    c_all = jnp.concatenate([c_prompt, c_sample], axis=0)
    c_pad = (-c_all.shape[0]) % 8
    c_all = jnp.pad(c_all, ((0, c_pad), (0, 0)))
    mod = _modulation(c_all, ada_w, ada_b)

    neg_ss = _neg_suffix_matrix()
    h_of = jnp.arange(W_A) // HEAD_DIM
    gmat = jnp.where(h_of[:, None] == h_of[None, :], 1.0 / HEAD_DIM, 0.0).astype(BF16)
    eye_h = jnp.eye(N_HEADS, dtype=bool)
    tri = jnp.tril(jnp.ones((CHUNK, CHUNK), bool))

    kt_pool = jnp.transpose(cache_k, (0, 1, 3, 4, 2)).reshape(DEPTH, n_pool, W_A, PAGE)
    vt_pool = jnp.transpose(cache_v, (0, 1, 3, 4, 2)).reshape(DEPTH, n_pool, W_A, PAGE)

    xp = x_prompt
    xs = jnp.transpose(x_sample, (1, 0, 2)).reshape(1, n_rows, D_MODEL)
    outs = {n: [] for n in ("kp", "vp", "ks", "vs", "cp", "cs", "fp", "fs", "gs")}
    zeros_conv = jnp.zeros((nb, 32, W_B), F32)
    zeros_ffn = jnp.zeros((nb, 8, D_FF), F32)

    def to_seq(a):
        return jnp.transpose(a.reshape(dt, db, a.shape[-1]), (1, 0, 2))

    def to_tm(a):
        return jnp.transpose(a, (1, 0, 2)).reshape(1, n_rows, a.shape[-1])

    for l in range(DEPTH):
        mod_p = [m[:, None, :] for m in jnp.split(mod[l, :nb], 6, axis=-1)]
        mod_s = [jnp.tile(m, (dt, 1))[None] for m in jnp.split(mod[l, nb:nb + db], 6, axis=-1)]
        w_in_b = w_in[l].astype(BF16)
        wo_b = w_o[l].astype(BF16)
        wg_b, wu_b, wd_b = ffn_wg[l].astype(BF16), ffn_wu[l].astype(BF16), ffn_wd[l].astype(BF16)
        qg = jnp.tile(q_norm_g[l], N_HEADS)[None]
        kg = jnp.tile(k_norm_g[l], N_HEADS)[None]
        n1, n2 = norm1_g[l][None], norm2_g[l][None]
        lng, lnb = sg_ln_g[l][None], sg_ln_b[l][None]
        cw = jnp.pad(conv_w[l], ((0, 1), (0, 0)))
        cb, clg, clb = conv_b[l][None], conv_ln_g[l][None], conv_ln_b[l][None]
        fcw = jnp.pad(ffn_conv_w[l], ((0, 5), (0, 0)))
        fcb = ffn_conv_b[l][None]

        sh1, sc1, g1, sh2, sc2, g2 = mod_p
        q, kf, vf, kb, vb, glu, gu, vsg = _pre(xp, sh1, sc1, n1, w_in_b, qg, kg, gmat, lng, lnb, tm=512)
        oa = _attn_prompt(q, _grouped_kt(kb, PROMPT_GK), vb, _bf16_pieces(sb_bias[l]), neg_ss, PROMPT_GK,
                          PROMPT_TQ)
        wm = jnp.where(tri[None], sg_w[l], 0.0).astype(BF16)
        bm = jnp.repeat(sg_b[l].T, HEAD_DIM, axis=1)
        xp, ctail = _mix(oa, glu, zeros_conv, cw, cb, clg, clb, gu, vsg, wm, bm, wo_b, xp, g1, step=1, tm=512)
        xp, ftail = _ffn(xp, sh2, sc2, g2, n2, wg_b, wu_b, wd_b, fcw, fcb, zeros_ffn, step=1, tm=256)
        outs["kp"].append(kf.reshape(nb, seq, N_HEADS, HEAD_DIM))
        outs["vp"].append(vf.reshape(nb, seq, N_HEADS, HEAD_DIM))
        outs["cp"].append(ctail[:, 32 - (CONV_W - 1):])
        outs["fp"].append(ftail[:, 8 - (FFN_CONV_W - 1):])

        sh1, sc1, g1, sh2, sc2, g2 = mod_s
        q, kf, vf, kb, vb, glu, gu, vsg = _pre(xs, sh1, sc1, n1, w_in_b, qg, kg, gmat, lng, lnb, tm=n_rows)
        q_seq = to_seq(q).reshape(db, dt, N_HEADS, HEAD_DIM)
        qbd = jnp.where(eye_h[None, None, :, :, None], q_seq[:, :, None, :, :],
                        jnp.zeros((), BF16)).reshape(db, dt * N_HEADS, W_A)
        bias_rows = jnp.broadcast_to(jnp.tile(sb_bias[l], dt)[:, None], (dt * N_HEADS, LANES))
        k_new = jnp.pad(to_seq(kb).astype(F32), ((0, 0), (0, 8 - dt), (0, 0)))
        v_new = jnp.pad(to_seq(vb).astype(F32), ((0, 0), (0, 8 - dt), (0, 0)))
        oa = to_tm(_attn_sample(qbd, bias_rows, k_new, v_new, neg_ss, kt_pool, vt_pool, page_table, l,
                                SAMPLE_PAGES_PER_STEP))
        hist = jnp.transpose(state_conv[l], (1, 0, 2)).reshape(1, (CONV_W - 1) * db, W_B)
        w4 = jnp.where(tri[:dt, :dt][None], sg_w[l][:, :dt, :dt], 0.0)
        wm = jnp.einsum("gts,bc->gtbsc", w4, jnp.eye(db, dtype=F32)).reshape(N_GROUPS_C, n_rows, n_rows)
        bm = jnp.repeat(jnp.repeat(sg_b[l][:, :dt].T, db, axis=0), HEAD_DIM, axis=1)
        xs, ctail = _mix(oa, glu, hist, cw, cb, clg, clb, gu, vsg, wm.astype(BF16), bm, wo_b, xs, g1,
                         step=db, tm=n_rows)
        fhist = jnp.transpose(state_ffn[l], (1, 0, 2)).reshape(1, (FFN_CONV_W - 1) * db, D_FF)
        xs, ftail = _ffn(xs, sh2, sc2, g2, n2, wg_b, wu_b, wd_b, fcw, fcb, fhist, step=db, tm=n_rows)
        outs["ks"].append(to_seq(kf).reshape(db, dt, N_HEADS, HEAD_DIM))
        outs["vs"].append(to_seq(vf).reshape(db, dt, N_HEADS, HEAD_DIM))
        outs["cs"].append(jnp.transpose(ctail.reshape(CONV_W - 1, db, W_B), (1, 0, 2)))
        outs["fs"].append(jnp.transpose(ftail.reshape(FFN_CONV_W - 1, db, D_FF), (1, 0, 2)))
        outs["gs"].append(to_seq(vsg))

    ys = to_seq(xs)
    st = {n: jnp.stack(v) for n, v in outs.items()}
    return (xp, ys, st["kp"], st["vp"], st["ks"], st["vs"], st["cp"], st["cs"], st["fp"], st["fs"], st["gs"])
```
